```python
import jax, jax.numpy as jnp
from jax import lax
import numpy as np

D_MODEL = 2048
BATCH = 4
SEQ = 8192
DEPTH = 2

HEAD_DIM = 128
ATTN_GROUPS = ((128, 1), (512, 4), (2048, 16))
N_GROUPS = 3
HEADS_PER_GROUP = 4
N_ATTN_HEADS = N_GROUPS * HEADS_PER_GROUP
ATTN_WIDTH = N_ATTN_HEADS * HEAD_DIM
ROPE_DIM = HEAD_DIM // 4
ROPE_THETA = 500000.0
CONV_CH = D_MODEL
CONV_WIDTH = 31
D_FF = 7168
N_EXPERTS = 8
TOP_K = 2
BLOCK = 128
EPS = 1e-6
NEG_INF = -1e30
IN_COLS = 3 * ATTN_WIDTH + 2 * CONV_CH + 2 * D_MODEL
N_DENSE = (DEPTH + 1) // 2
N_MOE = DEPTH // 2

kernel_name = "hybrid_dilated_attn_conformer_moe"


def rmsnorm(t, g):
    tf = t.astype(jnp.float32)
    y = tf * lax.rsqrt(jnp.mean(tf * tf, axis=-1, keepdims=True) + EPS)
    return (y * g.astype(jnp.float32)).astype(t.dtype)


def layernorm(t, g, b):
    tf = t.astype(jnp.float32)
    mu = jnp.mean(tf, axis=-1, keepdims=True)
    var = jnp.mean(jnp.square(tf - mu), axis=-1, keepdims=True)
    y = (tf - mu) * lax.rsqrt(var + EPS)
    return (y * g.astype(jnp.float32) + b.astype(jnp.float32)).astype(t.dtype)


def partial_rope(t, pos):
    half = ROPE_DIM // 2
    inv = jnp.power(jnp.float32(ROPE_THETA), -jnp.arange(half, dtype=jnp.float32) * 2.0 / ROPE_DIM)
    ang = pos[:, None] * inv[None, :]
    cos = jnp.cos(ang)[None, :, None, :]
    sin = jnp.sin(ang)[None, :, None, :]
    tf = t.astype(jnp.float32)
    x1 = tf[..., :half]
    x2 = tf[..., half:ROPE_DIM]
    out = jnp.concatenate([x1 * cos - x2 * sin, x2 * cos + x1 * sin, tf[..., ROPE_DIM:]], axis=-1)
    return out.astype(t.dtype)


def dilated_window_attention(q, k, v, window, dilation):
    b, s, h, e = q.shape
    span = window // dilation
    L = s // dilation
    pad = (-L) % BLOCK
    Lp = L + pad
    nb = Lp // BLOCK

    def strided(t):
        t = t.reshape(b, L, dilation, h, e).transpose(0, 2, 3, 1, 4)
        t = jnp.pad(t, ((0, 0), (0, 0), (0, 0), (0, pad), (0, 0)))
        return t.reshape(b, dilation, h, nb, BLOCK, e)

    def with_prev(t):
        prev = jnp.pad(t[:, :, :, :-1], ((0, 0), (0, 0), (0, 0), (1, 0), (0, 0), (0, 0)))
        return jnp.concatenate([prev, t], axis=4)

    qb = strided(q)
    kk = with_prev(strided(k))
    vv = with_prev(strided(v))
    scores = jnp.einsum('bdhnqe,bdhnke->bdhnqk', qb, kk,
                        preferred_element_type=jnp.float32) * (e ** -0.5)
    n_i = jnp.arange(nb)[:, None, None]
    q_i = jnp.arange(BLOCK)[None, :, None]
    k_i = jnp.arange(2 * BLOCK)[None, None, :]
    dist = BLOCK + q_i - k_i
    key_idx = (n_i - 1) * BLOCK + k_i
    mask = (dist >= 0) & (dist <= span) & (key_idx >= 0)
    scores = jnp.where(mask, scores, NEG_INF)
    lse = jax.nn.logsumexp(scores, axis=-1)
    p = jnp.exp(scores - lse[..., None]).astype(v.dtype)
    o = jnp.einsum('bdhnqk,bdhnke->bdhnqe', p, vv)
    o = o.reshape(b, dilation, h, Lp, e)[:, :, :, :L].transpose(0, 3, 1, 2, 4).reshape(b, s, h, e)
    lse = lse.reshape(b, dilation, h, Lp)[..., :L].transpose(0, 3, 1, 2).reshape(b, s, h)
    return o, lse


def swiglu(h, w1, w3, w2):
    return (jax.nn.silu(h @ w1) * (h @ w3)) @ w2


def moe_swiglu(h, router, w1, w3, w2):
    logits = jnp.einsum('bsd,de->bse', h, router, preferred_element_type=jnp.float32)
    top_v, top_i = lax.top_k(logits, TOP_K)
    top_w = jax.nn.softmax(top_v, axis=-1)
    gates = jnp.sum(jax.nn.one_hot(top_i, N_EXPERTS, dtype=jnp.float32) * top_w[..., None], axis=-2)
    gates = gates.astype(h.dtype)
    y = jnp.zeros_like(h)
    for e in range(N_EXPERTS):
        y = y + gates[..., e:e + 1] * swiglu(h, w1[e], w3[e], w2[e])
    return y


def hybrid_mixer(h, w_in, q_g, k_g, w_attn_o, conv_w, conv_b, ln_g, ln_b, w_conv_o, gate_b, w_out):
    b, s, _ = h.shape
    proj = h @ w_in
    o0 = 0
    q = proj[..., o0:o0 + ATTN_WIDTH].reshape(b, s, N_ATTN_HEADS, HEAD_DIM); o0 += ATTN_WIDTH
    k = proj[..., o0:o0 + ATTN_WIDTH].reshape(b, s, N_ATTN_HEADS, HEAD_DIM); o0 += ATTN_WIDTH
    v = proj[..., o0:o0 + ATTN_WIDTH].reshape(b, s, N_ATTN_HEADS, HEAD_DIM); o0 += ATTN_WIDTH
    c_val = proj[..., o0:o0 + CONV_CH]; o0 += CONV_CH
    c_gate = proj[..., o0:o0 + CONV_CH]; o0 += CONV_CH
    g_attn = proj[..., o0:o0 + D_MODEL]; o0 += D_MODEL
    g_conv = proj[..., o0:o0 + D_MODEL]

    pos = jnp.arange(s, dtype=jnp.float32)
    q = partial_rope(rmsnorm(q, q_g), pos)
    k = partial_rope(rmsnorm(k, k_g), pos)
    outs, lses = [], []
    for g, (window, dilation) in enumerate(ATTN_GROUPS):
        hs = slice(g * HEADS_PER_GROUP, (g + 1) * HEADS_PER_GROUP)
        o_g, lse_g = dilated_window_attention(q[:, :, hs], k[:, :, hs], v[:, :, hs], window, dilation)
        outs.append(o_g)
        lses.append(lse_g)
    alpha = jax.nn.softmax(jnp.stack(lses, axis=0), axis=0).astype(v.dtype)
    o = jnp.concatenate([outs[g] * alpha[g][..., None] for g in range(N_GROUPS)], axis=2)
    y_attn = o.reshape(b, s, ATTN_WIDTH) @ w_attn_o

    u = c_val * jax.nn.sigmoid(c_gate)
    u = lax.conv_general_dilated(u, conv_w.astype(u.dtype), window_strides=(1,),
                                 padding=((CONV_WIDTH - 1, 0),),
                                 dimension_numbers=('NWC', 'WIO', 'NWC'),
                                 feature_group_count=CONV_CH) + conv_b
    u = jax.nn.silu(layernorm(u, ln_g, ln_b))
    y_conv = u @ w_conv_o

    merged = jax.nn.sigmoid(g_attn + gate_b[:D_MODEL]) * y_attn + jax.nn.sigmoid(g_conv + gate_b[D_MODEL:]) * y_conv
    return merged @ w_out


def setup_inputs(seed: int = 0) -> dict:
    key = jax.random.key(seed)
    ks = jax.random.split(key, 24)
    f32 = jnp.float32
    nrm = lambda k, shape, scale: jax.random.normal(k, shape, f32) * scale
    return {
        "x": nrm(ks[0], (BATCH, SEQ, D_MODEL), 1.0),
        "norm_mix": 1.0 + nrm(ks[1], (DEPTH, D_MODEL), 0.02),
        "w_in": nrm(ks[2], (DEPTH, D_MODEL, IN_COLS), D_MODEL ** -0.5),
        "q_norm": 1.0 + nrm(ks[3], (DEPTH, HEAD_DIM), 0.02),
        "k_norm": 1.0 + nrm(ks[4], (DEPTH, HEAD_DIM), 0.02),
        "w_attn_o": nrm(ks[5], (DEPTH, ATTN_WIDTH, D_MODEL), ATTN_WIDTH ** -0.5),
        "conv_w": nrm(ks[6], (DEPTH, CONV_WIDTH, 1, CONV_CH), CONV_WIDTH ** -0.5),
        "conv_b": nrm(ks[7], (DEPTH, CONV_CH), 0.01),
        "conv_ln_g": 1.0 + nrm(ks[8], (DEPTH, CONV_CH), 0.02),
        "conv_ln_b": nrm(ks[9], (DEPTH, CONV_CH), 0.01),
        "w_conv_o": nrm(ks[10], (DEPTH, CONV_CH, D_MODEL), CONV_CH ** -0.5),
        "gate_b": nrm(ks[11], (DEPTH, 2 * D_MODEL), 0.01),
        "w_out": nrm(ks[12], (DEPTH, D_MODEL, D_MODEL), D_MODEL ** -0.5),
        "norm_ffn": 1.0 + nrm(ks[13], (DEPTH, D_MODEL), 0.02),
        "ffn_w1": nrm(ks[14], (N_DENSE, D_MODEL, D_FF), D_MODEL ** -0.5),
        "ffn_w3": nrm(ks[15], (N_DENSE, D_MODEL, D_FF), D_MODEL ** -0.5),
        "ffn_w2": nrm(ks[16], (N_DENSE, D_FF, D_MODEL), D_FF ** -0.5),
        "router": nrm(ks[17], (N_MOE, D_MODEL, N_EXPERTS), D_MODEL ** -0.5),
        "moe_w1": nrm(ks[18], (N_MOE, N_EXPERTS, D_MODEL, D_FF), D_MODEL ** -0.5),
        "moe_w3": nrm(ks[19], (N_MOE, N_EXPERTS, D_MODEL, D_FF), D_MODEL ** -0.5),
        "moe_w2": nrm(ks[20], (N_MOE, N_EXPERTS, D_FF, D_MODEL), D_FF ** -0.5),
    }


def reference(x, norm_mix, w_in, q_norm, k_norm, w_attn_o, conv_w, conv_b, conv_ln_g, conv_ln_b,
              w_conv_o, gate_b, w_out, norm_ffn, ffn_w1, ffn_w3, ffn_w2, router, moe_w1, moe_w3, moe_w2):
    for l in range(DEPTH):
        h = rmsnorm(x, norm_mix[l])
        x = x + hybrid_mixer(h, w_in[l], q_norm[l], k_norm[l], w_attn_o[l], conv_w[l], conv_b[l],
                             conv_ln_g[l], conv_ln_b[l], w_conv_o[l], gate_b[l], w_out[l])
        h = rmsnorm(x, norm_ffn[l])
        if l % 2 == 0:
            i = l // 2
            x = x + swiglu(h, ffn_w1[i], ffn_w3[i], ffn_w2[i])
        else:
            i = l // 2
            x = x + moe_swiglu(h, router[i], moe_w1[i], moe_w3[i], moe_w2[i])
    return x
```

```python
import functools

import jax
import jax.numpy as jnp
from jax import lax
from jax.experimental import pallas as pl
from jax.experimental.pallas import tpu as pltpu

F32 = jnp.float32
BF16 = jnp.bfloat16
I32 = jnp.int32

HEAD_DIM = 128
ATTN_GROUPS = ((128, 1), (512, 4), (2048, 16))
HEADS_PER_GROUP = 4
GROUP_WIDTH = HEADS_PER_GROUP * HEAD_DIM
ATTN_WIDTH = len(ATTN_GROUPS) * GROUP_WIDTH
ROPE_DIM = HEAD_DIM // 4
ROPE_THETA = 500000.0
CONV_WIDTH = 31
N_EXPERTS = 8
BLOCK = 128
EPS = 1e-6
NEG_INF = -1e30

LANES = 128
SUBLANES = 8
CONV_HALO = 32
VMEM_LIMIT_BYTES = 56 * 1024 * 1024


def _params(*semantics):
    return pltpu.CompilerParams(dimension_semantics=semantics, vmem_limit_bytes=VMEM_LIMIT_BYTES)


def _rms_rows(x, g):
    return x * lax.rsqrt(jnp.mean(x * x, axis=-1, keepdims=True) + EPS) * g


def _norm_rows_to(h_ref, x_ref, g_ref, row_chunk=256):
    n = x_ref.shape[0]
    row_chunk = min(row_chunk, n)

    def body(c, carry):
        rows = pl.ds(pl.multiple_of(c * row_chunk, row_chunk), row_chunk)
        h_ref[rows, :] = _rms_rows(x_ref[rows, :], g_ref[...]).astype(h_ref.dtype)
        return carry

    lax.fori_loop(0, n // row_chunk, body, 0)


def _norm_matmul_kernel(x_ref, g_ref, w_ref, o_ref, h_scr):
    @pl.when(pl.program_id(1) == 0)
    def _():
        _norm_rows_to(h_scr, x_ref, g_ref)

    o_ref[...] = jnp.dot(h_scr[...], w_ref[...], preferred_element_type=F32).astype(o_ref.dtype)


def norm_matmul(x, g, w, *, bm, bn):
    t, d = x.shape
    n = w.shape[1]
    bm, bn = min(bm, t), min(bn, n)
    return pl.pallas_call(
        _norm_matmul_kernel,
        grid=(t // bm, n // bn),
        in_specs=[
            pl.BlockSpec((bm, d), lambda i, j: (i, 0)),
            pl.BlockSpec((1, d), lambda i, j: (0, 0)),
            pl.BlockSpec((d, bn), lambda i, j: (0, j)),
        ],
        out_specs=pl.BlockSpec((bm, bn), lambda i, j: (i, j)),
        out_shape=jax.ShapeDtypeStruct((t, n), BF16),
        scratch_shapes=[pltpu.VMEM((bm, d), BF16)],
        compiler_params=_params("arbitrary", "arbitrary"),
        name="norm_matmul",
    )(x, g, w)


def _norm_rope(t, g, tab):
    y = _rms_rows(t.astype(F32), g)
    half = ROPE_DIM // 2
    return (y * tab[:, :LANES]
            + pltpu.roll(y, LANES - half, 1) * tab[:, LANES:2 * LANES]
            + pltpu.roll(y, half, 1) * tab[:, 2 * LANES:])


def _attn_kernel(q_ref, kc_ref, kp_ref, vc_ref, vp_ref, tc_ref, tp_ref, qg_ref, kg_ref, o_ref, lse_ref):
    has_prev = pl.program_id(2) > 0
    nsub = q_ref.shape[0] // BLOCK
    row = lax.broadcasted_iota(I32, (BLOCK, BLOCK), 0)
    col = lax.broadcasted_iota(I32, (BLOCK, BLOCK), 1)
    cur_mask = col <= row
    prev_mask = col >= row
    tab_c = tc_ref[...]
    tab_p = tp_ref[...]
    nt = (((1,), (1,)), ((), ()))
    lse_tiles = [jnp.zeros((BLOCK, LANES), F32) for _ in range(nsub)]
    for h in range(HEADS_PER_GROUP):
        hs = slice(h * HEAD_DIM, (h + 1) * HEAD_DIM)
        q = (_norm_rope(q_ref[:, hs], qg_ref[...], tab_c) * (HEAD_DIM ** -0.5)).astype(BF16)
        kc = _norm_rope(kc_ref[:, hs], kg_ref[...], tab_c).astype(BF16)
        kp = _norm_rope(kp_ref[:, hs], kg_ref[...], tab_p).astype(BF16)
        vc = vc_ref[:, hs]
        vp = vp_ref[:, hs]
        for j in range(nsub):
            rs = slice(j * BLOCK, (j + 1) * BLOCK)
            ps = slice((j - 1) * BLOCK, j * BLOCK)
            k_prev, v_prev = (kp, vp) if j == 0 else (kc[ps], vc[ps])
            s_cur = lax.dot_general(q[rs], kc[rs], nt, preferred_element_type=F32)
            s_prev = lax.dot_general(q[rs], k_prev, nt, preferred_element_type=F32)
            s_cur = jnp.where(cur_mask, s_cur, NEG_INF)
            s_prev = jnp.where(prev_mask, s_prev, NEG_INF)
            if j == 0:
                s_prev = jnp.where(has_prev, s_prev, NEG_INF)
            m = jnp.maximum(jnp.max(s_cur, axis=1, keepdims=True), jnp.max(s_prev, axis=1, keepdims=True))
            p_cur = jnp.exp(s_cur - m)
            p_prev = jnp.exp(s_prev - m)
            l = jnp.sum(p_cur, axis=1, keepdims=True) + jnp.sum(p_prev, axis=1, keepdims=True)
            o = (jnp.dot(p_cur.astype(BF16), vc[rs], preferred_element_type=F32)
                 + jnp.dot(p_prev.astype(BF16), v_prev, preferred_element_type=F32))
            o_ref[rs, hs] = (o / l).astype(o_ref.dtype)
            lse_tiles[j] = jnp.where(col == h, m + jnp.log(l), lse_tiles[j])
    for j in range(nsub):
        lse_ref[j * BLOCK:(j + 1) * BLOCK, :] = lse_tiles[j]


def dilated_attention(qkv, tab, qg, kg, *, group, batch, seq, lq_max=512):
    window, dil = ATTN_GROUPS[group]
    assert window // dil == BLOCK
    n_strided = seq // dil
    assert seq % dil == 0 and n_strided % BLOCK == 0
    lq = min(n_strided, lq_max)
    sub = lq // BLOCK
    ncol = qkv.shape[1] // GROUP_WIDTH
    ntab = tab.shape[1]
    qv = qkv.reshape(batch, n_strided, dil * qkv.shape[1])
    tv = tab.reshape(n_strided, dil * ntab)

    def prev_block(i):
        return jnp.maximum(i * sub - 1, 0)

    def cur(off):
        return pl.BlockSpec((None, lq, GROUP_WIDTH), lambda b, r, i: (b, i, r * ncol + off + group))

    def prev(off):
        return pl.BlockSpec((None, BLOCK, GROUP_WIDTH), lambda b, r, i: (b, prev_block(i), r * ncol + off + group))

    k_off = ATTN_WIDTH // GROUP_WIDTH
    v_off = 2 * k_off
    o, lse = pl.pallas_call(
        _attn_kernel,
        grid=(batch, dil, n_strided // lq),
        in_specs=[
            cur(0), cur(k_off), prev(k_off), cur(v_off), prev(v_off),
            pl.BlockSpec((lq, ntab), lambda b, r, i: (i, r)),
            pl.BlockSpec((BLOCK, ntab), lambda b, r, i: (prev_block(i), r)),
            pl.BlockSpec((1, HEAD_DIM), lambda b, r, i: (0, 0)),
            pl.BlockSpec((1, HEAD_DIM), lambda b, r, i: (0, 0)),
        ],
        out_specs=[
            pl.BlockSpec((None, lq, GROUP_WIDTH), lambda b, r, i: (b, i, r)),
            pl.BlockSpec((None, lq, LANES), lambda b, r, i: (b, i, r)),
        ],
        out_shape=[
            jax.ShapeDtypeStruct((batch, n_strided, dil * GROUP_WIDTH), BF16),
            jax.ShapeDtypeStruct((batch, n_strided, dil * LANES), F32),
        ],
        compiler_params=_params("arbitrary", "arbitrary", "arbitrary"),
        name=f"dilated_attention_g{group}",
    )(qv, qv, qv, qv, qv, tv, tv, qg, kg)
    t = batch * seq
    return o.reshape(t, GROUP_WIDTH), lse.reshape(t, LANES)


def rope_table(seq):
    half = ROPE_DIM // 2
    inv = jnp.power(jnp.float32(ROPE_THETA), -jnp.arange(half, dtype=F32) * 2.0 / ROPE_DIM)
    ang = jnp.arange(seq, dtype=F32)[:, None] * inv[None, :]
    cos, sin = jnp.cos(ang), jnp.sin(ang)
    ones = jnp.ones((seq, HEAD_DIM - ROPE_DIM), F32)
    zeros_tail = jnp.zeros((seq, HEAD_DIM - half), F32)
    c = jnp.concatenate([cos, cos, ones], axis=1)
    s_lo = jnp.concatenate([-sin, zeros_tail], axis=1)
    s_hi = jnp.concatenate([jnp.zeros((seq, half), F32), sin, jnp.zeros((seq, HEAD_DIM - ROPE_DIM), F32)], axis=1)
    return jnp.concatenate([c, s_lo, s_hi], axis=1)


CONV_ROW_CHUNK = 32
CONV_LANE_CHUNK = 512


def _conv_kernel(vc_ref, gc_ref, vp_ref, gp_ref, w_ref, cb_ref, lg_ref, lb_ref, o_ref, u_scr, y_scr):
    bs, ch = vc_ref.shape
    halo = vp_ref.shape[0]
    u_prev = vp_ref[...].astype(F32) * jax.nn.sigmoid(gp_ref[...].astype(F32))
    u_scr[0:halo, :] = jnp.where(pl.program_id(1) > 0, u_prev, 0.0)
    u_scr[halo:, :] = vc_ref[...].astype(F32) * jax.nn.sigmoid(gc_ref[...].astype(F32))
    base = halo - (CONV_WIDTH - 1)
    for r0 in range(0, bs, CONV_ROW_CHUNK):
        for c0 in range(0, ch, CONV_LANE_CHUNK):
            cs = slice(c0, c0 + CONV_LANE_CHUNK)
            acc = jnp.zeros((CONV_ROW_CHUNK, CONV_LANE_CHUNK), F32)
            for j in range(CONV_WIDTH):
                acc = acc + w_ref[j:j + 1, cs] * u_scr[r0 + base + j:r0 + base + j + CONV_ROW_CHUNK, cs]
            y_scr[r0:r0 + CONV_ROW_CHUNK, cs] = acc + cb_ref[:, cs]
    y = y_scr[...]
    mu = jnp.mean(y, axis=-1, keepdims=True)
    yc = y - mu
    var = jnp.mean(yc * yc, axis=-1, keepdims=True)
    z = yc * lax.rsqrt(var + EPS) * lg_ref[...] + lb_ref[...]
    o_ref[...] = (z * jax.nn.sigmoid(z)).astype(o_ref.dtype)


def conv_branch(rest, conv_w, conv_b, ln_g, ln_b, *, batch, seq, ch, bs):
    bs = min(bs, seq)
    rv = rest.reshape(batch, seq, rest.shape[1])
    nh = bs // CONV_HALO

    def cur(colblk):
        return pl.BlockSpec((None, bs, ch), lambda b, i: (b, i, colblk))

    def prev(colblk):
        return pl.BlockSpec((None, CONV_HALO, ch), lambda b, i: (b, jnp.maximum(i * nh - 1, 0), colblk))

    vec = pl.BlockSpec((1, ch), lambda b, i: (0, 0))
    out = pl.pallas_call(
        _conv_kernel,
        grid=(batch, seq // bs),
        in_specs=[cur(0), cur(1), prev(0), prev(1),
                  pl.BlockSpec((CONV_WIDTH, ch), lambda b, i: (0, 0)), vec, vec, vec],
        out_specs=pl.BlockSpec((None, bs, ch), lambda b, i: (b, i, 0)),
        out_shape=jax.ShapeDtypeStruct((batch, seq, ch), BF16),
        scratch_shapes=[pltpu.VMEM((CONV_HALO + bs, ch), F32), pltpu.VMEM((bs, ch), F32)],
        compiler_params=_params("arbitrary", "arbitrary"),
        name="conv_branch",
    )(rv, rv, rv, rv, conv_w, conv_b, ln_g, ln_b)
    return out.reshape(batch * seq, ch)


def _merge_kernel(o0_ref, o1_ref, o2_ref, l0_ref, l1_ref, l2_ref, u_ref, ga_ref, gc_ref,
                  wa_ref, wc_ref, ba_ref, bc_ref, out_ref, os_scr):
    @pl.when(pl.program_id(1) == 0)
    def _():
        l0, l1, l2 = l0_ref[...], l1_ref[...], l2_ref[...]
        m = jnp.maximum(jnp.maximum(l0, l1), l2)
        es = [jnp.exp(l0 - m), jnp.exp(l1 - m), jnp.exp(l2 - m)]
        inv = 1.0 / (es[0] + es[1] + es[2])
        for g, o_ref in enumerate((o0_ref, o1_ref, o2_ref)):
            alpha = es[g] * inv
            for h in range(HEADS_PER_GROUP):
                src = slice(h * HEAD_DIM, (h + 1) * HEAD_DIM)
                dst = slice((g * HEADS_PER_GROUP + h) * HEAD_DIM, (g * HEADS_PER_GROUP + h + 1) * HEAD_DIM)
                os_scr[:, dst] = (o_ref[:, src].astype(F32) * alpha[:, h:h + 1]).astype(os_scr.dtype)

    ya = jnp.dot(os_scr[...], wa_ref[...], preferred_element_type=F32)
    yc = jnp.dot(u_ref[...], wc_ref[...], preferred_element_type=F32)
    gate_a = jax.nn.sigmoid(ga_ref[...].astype(F32) + ba_ref[...])
    gate_c = jax.nn.sigmoid(gc_ref[...].astype(F32) + bc_ref[...])
    out_ref[...] = (gate_a * ya + gate_c * yc).astype(out_ref.dtype)


def gated_merge(outs, lses, u, rest, wa, wc, ba, bc, *, d, bm, bn):
    t = u.shape[0]
    ch = u.shape[1]
    bm, bn = min(bm, t), min(bn, d)
    ga_blk = (2 * ch) // bn
    gc_blk = (2 * ch + d) // bn
    row = lambda w: pl.BlockSpec((bm, w), lambda i, j: (i, 0))
    return pl.pallas_call(
        _merge_kernel,
        grid=(t // bm, d // bn),
        in_specs=[row(GROUP_WIDTH)] * 3 + [row(LANES)] * 3 + [
            row(ch),
            pl.BlockSpec((bm, bn), lambda i, j: (i, ga_blk + j)),
            pl.BlockSpec((bm, bn), lambda i, j: (i, gc_blk + j)),
            pl.BlockSpec((ATTN_WIDTH, bn), lambda i, j: (0, j)),
            pl.BlockSpec((ch, bn), lambda i, j: (0, j)),
            pl.BlockSpec((1, bn), lambda i, j: (0, j)),
            pl.BlockSpec((1, bn), lambda i, j: (0, j)),
        ],
        out_specs=pl.BlockSpec((bm, bn), lambda i, j: (i, j)),
        out_shape=jax.ShapeDtypeStruct((t, d), BF16),
        scratch_shapes=[pltpu.VMEM((bm, ATTN_WIDTH), BF16)],
        compiler_params=_params("arbitrary", "arbitrary"),
        name="gated_merge",
    )(*outs, *lses, u, rest, rest, wa, wc, ba, bc)


def _proj_residual_kernel(x_ref, a_ref, w_ref, o_ref):
    o_ref[...] = x_ref[...] + jnp.dot(a_ref[...], w_ref[...], preferred_element_type=F32)


def proj_residual(x, a, w, *, bm, bn):
    t, d = x.shape
    k = a.shape[1]
    bm, bn = min(bm, t), min(bn, d)
    return pl.pallas_call(
        _proj_residual_kernel,
        grid=(t // bm, d // bn),
        in_specs=[
            pl.BlockSpec((bm, bn), lambda i, j: (i, j)),
            pl.BlockSpec((bm, k), lambda i, j: (i, 0)),
            pl.BlockSpec((k, bn), lambda i, j: (0, j)),
        ],
        out_specs=pl.BlockSpec((bm, bn), lambda i, j: (i, j)),
        out_shape=jax.ShapeDtypeStruct((t, d), F32),
        compiler_params=_params("arbitrary", "arbitrary"),
        name="proj_residual",
    )(x, a, w)


def _swiglu_step(h, w1_ref, w3_ref, w2_ref):
    a = jnp.dot(h, w1_ref[...], preferred_element_type=F32)
    b = jnp.dot(h, w3_ref[...], preferred_element_type=F32)
    act = (a * jax.nn.sigmoid(a) * b).astype(BF16)
    return jnp.dot(act, w2_ref[...], preferred_element_type=F32)


def _ffn_kernel(x_ref, g_ref, w1_ref, w3_ref, w2_ref, o_ref, h_scr):
    @pl.when(pl.program_id(1) == 0)
    def _():
        _norm_rows_to(h_scr, x_ref, g_ref)
        o_ref[...] = x_ref[...]

    o_ref[...] += _swiglu_step(h_scr[...], w1_ref, w3_ref, w2_ref)


def dense_ffn(x, g, w1, w3, w2, *, bm, bf):
    t, d = x.shape
    f = w1.shape[1]
    bm, bf = min(bm, t), min(bf, f)
    return pl.pallas_call(
        _ffn_kernel,
        grid=(t // bm, f // bf),
        in_specs=[
            pl.BlockSpec((bm, d), lambda i, k: (i, 0)),
            pl.BlockSpec((1, d), lambda i, k: (0, 0)),
            pl.BlockSpec((d, bf), lambda i, k: (0, k)),
            pl.BlockSpec((d, bf), lambda i, k: (0, k)),
            pl.BlockSpec((bf, d), lambda i, k: (k, 0)),
        ],
        out_specs=pl.BlockSpec((bm, d), lambda i, k: (i, 0)),
        out_shape=jax.ShapeDtypeStruct((t, d), F32),
        scratch_shapes=[pltpu.VMEM((bm, d), BF16)],
        compiler_params=_params("arbitrary", "arbitrary"),
        name="dense_ffn",
    )(x, g, w1, w3, w2)


def _router_kernel(x_ref, g_ref, r_ref, tri_ref, hn_ref, dest_ref, gate_ref, cnt_ref, carry_scr, *, capacity):
    @pl.when(pl.program_id(0) == 0)
    def _():
        carry_scr[...] = jnp.zeros_like(carry_scr)

    h = _rms_rows(x_ref[...], g_ref[...])
    hn_ref[...] = h
    logits = jnp.dot(h, r_ref[...], preferred_element_type=F32, precision=lax.Precision.HIGHEST)
    lane = lax.broadcasted_iota(I32, logits.shape, 1)
    logits = jnp.where(lane < N_EXPERTS, logits, NEG_INF)
    m1 = jnp.max(logits, axis=1, keepdims=True)
    i1 = jnp.min(jnp.where(logits == m1, lane, LANES), axis=1, keepdims=True)
    hot1 = lane == i1
    rest = jnp.where(hot1, NEG_INF, logits)
    m2 = jnp.max(rest, axis=1, keepdims=True)
    i2 = jnp.min(jnp.where(rest == m2, lane, LANES), axis=1, keepdims=True)
    hot2 = lane == i2
    e = jnp.exp(m2 - m1)
    w1 = 1.0 / (1.0 + e)
    w2 = e * w1
    sel = jnp.where(hot1, 1.0, 0.0) + jnp.where(hot2, 1.0, 0.0)
    rank = carry_scr[...] + jnp.dot(tri_ref[...], sel.astype(BF16), preferred_element_type=F32)
    r1 = jnp.sum(jnp.where(hot1, rank, 0.0), axis=1, keepdims=True).astype(I32)
    r2 = jnp.sum(jnp.where(hot2, rank, 0.0), axis=1, keepdims=True).astype(I32)
    carry_scr[...] = carry_scr[...] + jnp.sum(sel, axis=0, keepdims=True)
    dest_ref[...] = jnp.where(lane == 0, i1 * capacity + r1, jnp.where(lane == 1, i2 * capacity + r2, 0))
    gate_ref[...] = jnp.where(lane == 0, w1, jnp.where(lane == 1, w2, 0.0))
    cnt_ref[...] = jnp.broadcast_to(carry_scr[...], cnt_ref.shape)


def moe_router(x, g, router, *, bm):
    t, d = x.shape
    bm = min(bm, t)
    r_pad = jnp.zeros((d, LANES), F32).at[:, :N_EXPERTS].set(router)
    tri = (lax.broadcasted_iota(I32, (bm, bm), 0) > lax.broadcasted_iota(I32, (bm, bm), 1)).astype(BF16)
    return pl.pallas_call(
        functools.partial(_router_kernel, capacity=t),
        grid=(t // bm,),
        in_specs=[
            pl.BlockSpec((bm, d), lambda i: (i, 0)),
            pl.BlockSpec((1, d), lambda i: (0, 0)),
            pl.BlockSpec((d, LANES), lambda i: (0, 0)),
            pl.BlockSpec((bm, bm), lambda i: (0, 0)),
        ],
        out_specs=[
            pl.BlockSpec((bm, d), lambda i: (i, 0)),
            pl.BlockSpec((bm, LANES), lambda i: (i, 0)),
            pl.BlockSpec((bm, LANES), lambda i: (i, 0)),
            pl.BlockSpec((8, LANES), lambda i: (0, 0)),
        ],
        out_shape=[
            jax.ShapeDtypeStruct((t, d), F32),
            jax.ShapeDtypeStruct((t, LANES), I32),
            jax.ShapeDtypeStruct((t, LANES), F32),
            jax.ShapeDtypeStruct((8, LANES), F32),
        ],
        scratch_shapes=[pltpu.VMEM((1, LANES), F32)],
        compiler_params=_params("arbitrary"),
        name="moe_router",
    )(x, g, r_pad, tri)


def _row_copy(src_ref, src_row, dst_ref, dst_row, sem):
    return pltpu.make_async_copy(src_ref.at[pl.ds(src_row, 1)], dst_ref.at[pl.ds(dst_row, 1)], sem)


def _dispatch_kernel(cnt_ref, d0_ref, d1_ref, hn_ref, xs_ref, zero_scr, sem, pad_sem, *, capacity, row_tile):
    bm = hn_ref.shape[0]

    def pad_copies(do):
        for e in range(N_EXPERTS):
            cnt = cnt_ref[e]
            pad = (-cnt) & (row_tile - 1)
            head = pad & (SUBLANES - 1)
            off = e * capacity + cnt
            for b in range(SUBLANES - 1):
                @pl.when(b < head)
                def _(off=off, b=b):
                    do(pltpu.make_async_copy(zero_scr.at[pl.ds(0, 1)], xs_ref.at[pl.ds(off + b, 1)], pad_sem))
            off = off + head
            size = SUBLANES
            while size < row_tile:
                @pl.when((pad & size) != 0)
                def _(off=off, size=size):
                    dst = xs_ref.at[pl.ds(pl.multiple_of(off, SUBLANES), size)]
                    do(pltpu.make_async_copy(zero_scr.at[pl.ds(0, size)], dst, pad_sem))
                off = off + (pad & size)
                size *= 2

    @pl.when(pl.program_id(0) == 0)
    def _():
        zero_scr[...] = jnp.zeros_like(zero_scr)
        pad_copies(lambda cp: cp.start())
        pad_copies(lambda cp: cp.wait())

    def issue(t, carry):
        _row_copy(hn_ref, t, xs_ref, d0_ref[0, 0, t], sem).start()
        _row_copy(hn_ref, t, xs_ref, d1_ref[0, 0, t], sem).start()
        return carry

    def drain(t, carry):
        _row_copy(hn_ref, 0, xs_ref, 0, sem).wait()
        _row_copy(hn_ref, 0, xs_ref, 0, sem).wait()
        return carry

    lax.fori_loop(0, bm, issue, 0)
    lax.fori_loop(0, bm, drain, 0)


def moe_dispatch(counts, dest0, dest1, hn, *, bm, row_tile):
    t, d = hn.shape
    bm = min(bm, t)
    nt = t // bm
    idx_spec = pl.BlockSpec((1, 1, bm), lambda i, cnt: (i, 0, 0), memory_space=pltpu.SMEM)
    return pl.pallas_call(
        functools.partial(_dispatch_kernel, capacity=t, row_tile=row_tile),
        grid_spec=pltpu.PrefetchScalarGridSpec(
            num_scalar_prefetch=1,
            grid=(nt,),
            in_specs=[idx_spec, idx_spec, pl.BlockSpec((bm, d), lambda i, cnt: (i, 0))],
            out_specs=pl.BlockSpec(memory_space=pl.ANY),
            scratch_shapes=[pltpu.VMEM((row_tile // 2, d), F32),
                            pltpu.SemaphoreType.DMA(()), pltpu.SemaphoreType.DMA(())],
        ),
        out_shape=jax.ShapeDtypeStruct((N_EXPERTS * t, d), F32),
        compiler_params=_params("arbitrary"),
        name="moe_dispatch",
    )(counts, dest0.reshape(nt, 1, bm), dest1.reshape(nt, 1, bm), hn)


def _moe_ffn_kernel(te_ref, tb_ref, tv_ref, x_ref, w1_ref, w3_ref, w2_ref, o_ref, h_scr):
    @pl.when(tv_ref[pl.program_id(0)] != 0)
    def _():
        @pl.when(pl.program_id(1) == 0)
        def _():
            h_scr[...] = x_ref[...].astype(h_scr.dtype)
            o_ref[...] = jnp.zeros_like(o_ref)

        o_ref[...] += _swiglu_step(h_scr[...], w1_ref, w3_ref, w2_ref)


def moe_grouped_ffn(tile_expert, tile_block, tile_valid, xs, w1, w3, w2, *, row_tile, bf):
    rows, d = xs.shape
    f = w1.shape[2]
    bf = min(bf, f)
    nf = f // bf
    n_steps = tile_expert.shape[0]

    def fk(s, k, tv):
        return jnp.where(tv[s] != 0, k, nf - 1)

    return pl.pallas_call(
        _moe_ffn_kernel,
        grid_spec=pltpu.PrefetchScalarGridSpec(
            num_scalar_prefetch=3,
            grid=(n_steps, nf),
            in_specs=[
                pl.BlockSpec((row_tile, d), lambda s, k, te, tb, tv: (tb[s], 0)),
                pl.BlockSpec((None, d, bf), lambda s, k, te, tb, tv: (te[s], 0, fk(s, k, tv))),
                pl.BlockSpec((None, d, bf), lambda s, k, te, tb, tv: (te[s], 0, fk(s, k, tv))),
                pl.BlockSpec((None, bf, d), lambda s, k, te, tb, tv: (te[s], fk(s, k, tv), 0)),
            ],
            out_specs=pl.BlockSpec((row_tile, d), lambda s, k, te, tb, tv: (tb[s], 0)),
            scratch_shapes=[pltpu.VMEM((row_tile, d), BF16)],
        ),
        out_shape=jax.ShapeDtypeStruct((rows, d), F32),
        compiler_params=_params("arbitrary", "arbitrary"),
        name="moe_grouped_ffn",
    )(tile_expert, tile_block, tile_valid, xs, w1, w3, w2)


def _combine_kernel(d0_ref, d1_ref, x_ref, gate_ref, ys_ref, o_ref, buf, sem):
    bm = x_ref.shape[0]

    def issue(t, carry):
        _row_copy(ys_ref, d0_ref[0, 0, t], buf.at[0], t, sem).start()
        _row_copy(ys_ref, d1_ref[0, 0, t], buf.at[1], t, sem).start()
        return carry

    def drain(t, carry):
        _row_copy(ys_ref, 0, buf.at[0], 0, sem).wait()
        _row_copy(ys_ref, 0, buf.at[1], 0, sem).wait()
        return carry

    lax.fori_loop(0, bm, issue, 0)
    lax.fori_loop(0, bm, drain, 0)
    gates = gate_ref[...]
    o_ref[...] = x_ref[...] + gates[:, 0:1] * buf[0] + gates[:, 1:2] * buf[1]


def moe_combine(dest0, dest1, x, gates, ys, *, bm):
    t, d = x.shape
    bm = min(bm, t)
    nt = t // bm
    idx_spec = pl.BlockSpec((1, 1, bm), lambda i: (i, 0, 0), memory_space=pltpu.SMEM)
    return pl.pallas_call(
        _combine_kernel,
        grid=(nt,),
        in_specs=[idx_spec, idx_spec,
                  pl.BlockSpec((bm, d), lambda i: (i, 0)),
                  pl.BlockSpec((bm, LANES), lambda i: (i, 0)),
                  pl.BlockSpec(memory_space=pl.ANY)],
        out_specs=pl.BlockSpec((bm, d), lambda i: (i, 0)),
        out_shape=jax.ShapeDtypeStruct((t, d), F32),
        scratch_shapes=[pltpu.VMEM((2, bm, d), F32), pltpu.SemaphoreType.DMA(())],
        compiler_params=_params("arbitrary"),
        name="moe_combine",
    )(dest0.reshape(nt, 1, bm), dest1.reshape(nt, 1, bm), x, gates, ys)


def _tile_tables(counts, *, capacity, row_tile, n_steps):
    tiles = (counts + row_tile - 1) // row_tile
    ends = jnp.cumsum(tiles)
    total = ends[-1]
    step = jnp.minimum(jnp.arange(n_steps, dtype=I32), total - 1)
    expert = jnp.sum((step[:, None] >= ends[None, :]).astype(I32), axis=1)
    block = expert * (capacity // row_tile) + step - (ends - tiles)[expert]
    valid = (jnp.arange(n_steps, dtype=I32) < total).astype(I32)
    return expert.astype(I32), block.astype(I32), valid


def moe_ffn(x, g, router, w1, w3, w2, *, bm, row_tile, bf):
    t, d = x.shape
    row_tile = min(row_tile, t)
    hn, dest, gates, cnt = moe_router(x, g, router, bm=bm)
    counts = cnt[0, :N_EXPERTS].astype(I32)
    dest0, dest1 = dest[:, 0], dest[:, 1]
    n_steps = (2 * t) // row_tile + N_EXPERTS
    tile_expert, tile_block, tile_valid = _tile_tables(counts, capacity=t, row_tile=row_tile, n_steps=n_steps)
    xs = moe_dispatch(counts, dest0, dest1, hn, bm=bm, row_tile=row_tile)
    ys = moe_grouped_ffn(tile_expert, tile_block, tile_valid, xs, w1, w3, w2, row_tile=row_tile, bf=bf)
    return moe_combine(dest0, dest1, x, gates, ys, bm=bm)


def kernel(x, norm_mix, w_in, q_norm, k_norm, w_attn_o, conv_w, conv_b, conv_ln_g, conv_ln_b, w_conv_o, gate_b,
           w_out, norm_ffn, ffn_w1, ffn_w3, ffn_w2, router, moe_w1, moe_w3, moe_w2):
    batch, seq, d = x.shape
    depth = w_in.shape[0]
    ch = conv_w.shape[-1]
    t = batch * seq
    qkv_cols = 3 * ATTN_WIDTH
    tab = rope_table(seq)
    row = lambda v: v.reshape(1, -1).astype(F32)
    xt = x.reshape(t, d)
    for l in range(depth):
        w_in_l = w_in[l].astype(BF16)
        g_mix = row(norm_mix[l])
        qkv = norm_matmul(xt, g_mix, w_in_l[:, :qkv_cols], bm=1024, bn=ATTN_WIDTH)
        rest = norm_matmul(xt, g_mix, w_in_l[:, qkv_cols:], bm=1024, bn=1024)
        outs, lses = [], []
        for grp in range(len(ATTN_GROUPS)):
            o_g, lse_g = dilated_attention(qkv, tab, row(q_norm[l]), row(k_norm[l]), group=grp, batch=batch, seq=seq)
            outs.append(o_g)
            lses.append(lse_g)
        u = conv_branch(rest, conv_w[l].reshape(CONV_WIDTH, ch), row(conv_b[l]), row(conv_ln_g[l]),
                        row(conv_ln_b[l]), batch=batch, seq=seq, ch=ch, bs=256)
        merged = gated_merge(outs, lses, u, rest, w_attn_o[l].astype(BF16), w_conv_o[l].astype(BF16),
                             row(gate_b[l, :d]), row(gate_b[l, d:]), d=d, bm=512, bn=1024)
        xt = proj_residual(xt, merged, w_out[l].astype(BF16), bm=1024, bn=1024)
        g_ffn = row(norm_ffn[l])
        i = l // 2
        if l % 2 == 0:
            xt = dense_ffn(xt, g_ffn, ffn_w1[i].astype(BF16), ffn_w3[i].astype(BF16), ffn_w2[i].astype(BF16),
                           bm=512, bf=1024)
        else:
            xt = moe_ffn(xt, g_ffn, router[i], moe_w1[i].astype(BF16), moe_w3[i].astype(BF16),
                         moe_w2[i].astype(BF16), bm=512, row_tile=512, bf=1024)
    return xt.reshape(batch, seq, d)
```

```python
import functools

import jax
import jax.numpy as jnp
from jax import lax
from jax.experimental import pallas as pl
from jax.experimental.pallas import tpu as pltpu

F32 = jnp.float32
BF16 = jnp.bfloat16
I32 = jnp.int32

HEAD_DIM = 128
ATTN_GROUPS = ((128, 1), (512, 4), (2048, 16))
HEADS_PER_GROUP = 4
GROUP_WIDTH = HEADS_PER_GROUP * HEAD_DIM
ATTN_WIDTH = len(ATTN_GROUPS) * GROUP_WIDTH
ROPE_DIM = HEAD_DIM // 4
ROPE_THETA = 500000.0
CONV_WIDTH = 31
N_EXPERTS = 8
BLOCK = 128
EPS = 1e-6
NEG_INF = -1e30

LANES = 128
SUBLANES = 8
CONV_HALO = 32
VMEM_LIMIT_BYTES = 56 * 1024 * 1024


def _params(*semantics):
    return pltpu.CompilerParams(dimension_semantics=semantics, vmem_limit_bytes=VMEM_LIMIT_BYTES)


def _rms_rows(x, g):
    return x * lax.rsqrt(jnp.mean(x * x, axis=-1, keepdims=True) + EPS) * g


def _norm_rows_to(h_ref, x_ref, g_ref, row_chunk=256):
    n = x_ref.shape[0]
    row_chunk = min(row_chunk, n)

    def body(c, carry):
        rows = pl.ds(pl.multiple_of(c * row_chunk, row_chunk), row_chunk)
        h_ref[rows, :] = _rms_rows(x_ref[rows, :], g_ref[...]).astype(h_ref.dtype)
        return carry

    lax.fori_loop(0, n // row_chunk, body, 0)


def _norm_permute_kernel(x_ref, g_ref, h_ref, *rest, dilations, row_chunk=256):
    perm_refs, scr = rest[:-1], rest[-1]
    bm, dm = x_ref.shape
    nl = dm // LANES
    row_chunk = min(row_chunk, bm)

    def body(c, carry):
        rows = pl.ds(pl.multiple_of(c * row_chunk, row_chunk), row_chunk)
        h = _rms_rows(x_ref[rows, :], g_ref[...])
        h_ref[rows, :] = h.astype(h_ref.dtype)
        for l in range(nl):
            scr[l, rows, :] = h[:, l * LANES:(l + 1) * LANES]
        return carry

    lax.fori_loop(0, bm // row_chunk, body, 0)
    for p_ref, dil in zip(perm_refs, dilations):
        n = bm // dil
        for r in range(dil):
            for l in range(nl):
                p_ref[r, :, l * LANES:(l + 1) * LANES] = scr[l, pl.ds(r, n, stride=dil), :].astype(p_ref.dtype)


def norm_permute(x, g, *, batch, seq, dilations, bm):
    t, d = x.shape
    bm = min(bm, seq)
    xv = x.reshape(batch, seq, d)
    outs = pl.pallas_call(
        functools.partial(_norm_permute_kernel, dilations=dilations),
        grid=(batch, seq // bm),
        in_specs=[pl.BlockSpec((None, bm, d), lambda b, i: (b, i, 0)),
                  pl.BlockSpec((1, d), lambda b, i: (0, 0))],
        out_specs=[pl.BlockSpec((None, bm, d), lambda b, i: (b, i, 0))] + [
            pl.BlockSpec((None, dil, bm // dil, d), lambda b, i: (b, 0, i, 0)) for dil in dilations],
        out_shape=[jax.ShapeDtypeStruct((batch, seq, d), BF16)] + [
            jax.ShapeDtypeStruct((batch, dil, seq // dil, d), BF16) for dil in dilations],
        scratch_shapes=[pltpu.VMEM((d // LANES, bm, LANES), F32)],
        compiler_params=_params("arbitrary", "arbitrary"),
        name="norm_permute",
    )(xv, g)
    return [o.reshape(t, d) for o in outs]


def _matmul_kernel(a_ref, w_ref, o_ref):
    o_ref[...] = jnp.dot(a_ref[...], w_ref[...], preferred_element_type=F32).astype(o_ref.dtype)


def matmul(a, w, *, bm, bn):
    t, k = a.shape
    n = w.shape[1]
    bm, bn = min(bm, t), min(bn, n)
    return pl.pallas_call(
        _matmul_kernel,
        grid=(t // bm, n // bn),
        in_specs=[pl.BlockSpec((bm, k), lambda i, j: (i, 0)),
                  pl.BlockSpec((k, bn), lambda i, j: (0, j))],
        out_specs=pl.BlockSpec((bm, bn), lambda i, j: (i, j)),
        out_shape=jax.ShapeDtypeStruct((t, n), BF16),
        compiler_params=_params("arbitrary", "arbitrary"),
        name="matmul",
    )(a, w)


def _norm_rope(t, g, tab):
    y = _rms_rows(t, g)
    half = ROPE_DIM // 2
    return (y * tab[:, :LANES]
            + pltpu.roll(y, LANES - half, 1) * tab[:, LANES:2 * LANES]
            + pltpu.roll(y, half, 1) * tab[:, 2 * LANES:])


def _qkv_proj_kernel(h_ref, w_ref, tab_ref, qg_ref, kg_ref, o_ref):
    tab = tab_ref[...]
    pair = 2 * HEAD_DIM
    for c0 in range(0, o_ref.shape[1], pair):
        acc = jnp.dot(h_ref[...], w_ref[:, c0:c0 + pair], preferred_element_type=F32)
        for hc in range(c0, c0 + pair, HEAD_DIM):
            cs = slice(hc, hc + HEAD_DIM)
            a = acc[:, hc - c0:hc - c0 + HEAD_DIM]
            if hc < GROUP_WIDTH:
                o_ref[:, cs] = (_norm_rope(a, qg_ref[...], tab) * (HEAD_DIM ** -0.5)).astype(o_ref.dtype)
            elif hc < 2 * GROUP_WIDTH:
                o_ref[:, cs] = _norm_rope(a, kg_ref[...], tab).astype(o_ref.dtype)
            else:
                o_ref[:, cs] = a.astype(o_ref.dtype)


def qkv_proj(h, w, tab, qg, kg, *, seq, bm):
    t, d = h.shape
    n = w.shape[1]
    bm = min(bm, seq)
    per_batch = seq // bm
    return pl.pallas_call(
        _qkv_proj_kernel,
        grid=(t // bm,),
        in_specs=[pl.BlockSpec((bm, d), lambda i: (i, 0)),
                  pl.BlockSpec((d, n), lambda i: (0, 0)),
                  pl.BlockSpec((bm, tab.shape[1]), lambda i: (i % per_batch, 0)),
                  pl.BlockSpec((1, HEAD_DIM), lambda i: (0, 0)),
                  pl.BlockSpec((1, HEAD_DIM), lambda i: (0, 0))],
        out_specs=pl.BlockSpec((bm, n), lambda i: (i, 0)),
        out_shape=jax.ShapeDtypeStruct((t, n), BF16),
        compiler_params=_params("arbitrary"),
        name="qkv_proj",
    )(h, w, tab, qg, kg)


def rope_table(seq, dil):
    half = ROPE_DIM // 2
    inv = jnp.power(jnp.float32(ROPE_THETA), -jnp.arange(half, dtype=F32) * 2.0 / ROPE_DIM)
    pos = (jnp.arange(seq // dil, dtype=I32)[None, :] * dil + jnp.arange(dil, dtype=I32)[:, None]).reshape(seq)
    ang = pos.astype(F32)[:, None] * inv[None, :]
    cos, sin = jnp.cos(ang), jnp.sin(ang)
    ones = jnp.ones((seq, HEAD_DIM - ROPE_DIM), F32)
    zeros_tail = jnp.zeros((seq, HEAD_DIM - half), F32)
    c = jnp.concatenate([cos, cos, ones], axis=1)
    s_lo = jnp.concatenate([-sin, zeros_tail], axis=1)
    s_hi = jnp.concatenate([jnp.zeros((seq, half), F32), sin, jnp.zeros((seq, HEAD_DIM - ROPE_DIM), F32)], axis=1)
    return jnp.concatenate([c, s_lo, s_hi], axis=1)


def _attn_kernel(q_ref, kc_ref, kp_ref, vc_ref, vp_ref, o_ref, lse_ref, o_scr, lse_scr):
    has_prev = pl.program_id(1) > 0
    dil, lq = q_ref.shape[0], q_ref.shape[1]
    nsub = lq // BLOCK
    row = lax.broadcasted_iota(I32, (BLOCK, BLOCK), 0)
    col = lax.broadcasted_iota(I32, (BLOCK, BLOCK), 1)
    cur_mask = col <= row
    prev_mask = col >= row
    nt = (((1,), (1,)), ((), ()))

    def residue(r, carry):
        for j in range(nsub):
            rs = slice(j * BLOCK, (j + 1) * BLOCK)
            ps = slice((j - 1) * BLOCK, j * BLOCK)
            tok = pl.ds(j * BLOCK * dil + r, BLOCK, stride=dil)
            lse_tile = jnp.zeros((BLOCK, LANES), F32)
            for h in range(HEADS_PER_GROUP):
                hs = slice(h * HEAD_DIM, (h + 1) * HEAD_DIM)
                q = q_ref[r, rs, hs]
                if j == 0:
                    k_prev, v_prev = kp_ref[r, :, hs], vp_ref[r, :, hs]
                else:
                    k_prev, v_prev = kc_ref[r, ps, hs], vc_ref[r, ps, hs]
                s_cur = lax.dot_general(q, kc_ref[r, rs, hs], nt, preferred_element_type=F32)
                s_prev = lax.dot_general(q, k_prev, nt, preferred_element_type=F32)
                s_cur = jnp.where(cur_mask, s_cur, NEG_INF)
                s_prev = jnp.where(prev_mask, s_prev, NEG_INF)
                if j == 0:
                    s_prev = jnp.where(has_prev, s_prev, NEG_INF)
                m = jnp.maximum(jnp.max(s_cur, axis=1, keepdims=True), jnp.max(s_prev, axis=1, keepdims=True))
                p_cur = jnp.exp(s_cur - m)
                p_prev = jnp.exp(s_prev - m)
                l = jnp.sum(p_cur, axis=1, keepdims=True) + jnp.sum(p_prev, axis=1, keepdims=True)
                o = (jnp.dot(p_cur.astype(BF16), vc_ref[r, rs, hs], preferred_element_type=F32)
                     + jnp.dot(p_prev.astype(BF16), v_prev, preferred_element_type=F32))
                o_scr[h, tok, :] = o / l
                lse_tile = jnp.where(col == h, m + jnp.log(l), lse_tile)
            lse_scr[0, tok, :] = lse_tile
        return carry

    if dil == 1:
        residue(0, 0)
    else:
        lax.fori_loop(0, dil, residue, 0)
    for h in range(HEADS_PER_GROUP):
        o_ref[:, h * HEAD_DIM:(h + 1) * HEAD_DIM] = o_scr[h].astype(o_ref.dtype)
    lse_ref[...] = lse_scr[0]


def dilated_attention(qkv, *, group, batch, seq, lq):
    window, dil = ATTN_GROUPS[group]
    assert window // dil == BLOCK
    n_strided = seq // dil
    assert seq % dil == 0 and n_strided % BLOCK == 0
    lq = min(n_strided, lq)
    sub = lq // BLOCK
    tile = lq * dil
    qv = qkv.reshape(batch, dil, n_strided, qkv.shape[1])

    def cur(colblk):
        return pl.BlockSpec((None, dil, lq, GROUP_WIDTH), lambda b, i: (b, 0, i, colblk))

    def prev(colblk):
        return pl.BlockSpec((None, dil, BLOCK, GROUP_WIDTH), lambda b, i: (b, 0, jnp.maximum(i * sub - 1, 0), colblk))

    t = batch * seq
    per_batch = seq // tile
    return pl.pallas_call(
        _attn_kernel,
        grid=(batch, per_batch),
        in_specs=[cur(0), cur(1), prev(1), cur(2), prev(2)],
        out_specs=[pl.BlockSpec((tile, GROUP_WIDTH), lambda b, i: (b * per_batch + i, 0)),
                   pl.BlockSpec((tile, LANES), lambda b, i: (b * per_batch + i, 0))],
        out_shape=[jax.ShapeDtypeStruct((t, GROUP_WIDTH), BF16), jax.ShapeDtypeStruct((t, LANES), F32)],
        scratch_shapes=[pltpu.VMEM((HEADS_PER_GROUP, tile, HEAD_DIM), F32), pltpu.VMEM((1, tile, LANES), F32)],
        compiler_params=_params("arbitrary", "arbitrary"),
        name=f"dilated_attention_g{group}",
    )(qv, qv, qv, qv, qv)


CONV_ROW_CHUNK = 64
CONV_LANE_CHUNK = 256


def _conv_kernel(vc_ref, gc_ref, vp_ref, gp_ref, w_ref, cb_ref, lg_ref, lb_ref, o_ref, u_scr, y_scr):
    bs, ch = vc_ref.shape
    halo = vp_ref.shape[0]
    u_prev = vp_ref[...].astype(F32) * jax.nn.sigmoid(gp_ref[...].astype(F32))
    u_scr[0:halo, :] = jnp.where(pl.program_id(1) > 0, u_prev, 0.0)
    u_scr[halo:, :] = vc_ref[...].astype(F32) * jax.nn.sigmoid(gc_ref[...].astype(F32))
    rc, lc = CONV_ROW_CHUNK, CONV_LANE_CHUNK
    for r0 in range(0, bs, rc):
        for c0 in range(0, ch, lc):
            cs = slice(c0, c0 + lc)
            acc = None
            for b in range(SUBLANES):
                part = None
                for a in range(-(-CONV_WIDTH // SUBLANES)):
                    k = SUBLANES * a + b
                    if k >= CONV_WIDTH:
                        continue
                    j = CONV_WIDTH - 1 - k
                    lo = halo + r0 - SUBLANES * (a + 1)
                    term = w_ref[j:j + 1, cs] * u_scr[lo:lo + rc + SUBLANES, cs]
                    part = term if part is None else part + term
                shifted = part[SUBLANES - b:SUBLANES - b + rc]
                acc = shifted if acc is None else acc + shifted
            y_scr[r0:r0 + rc, cs] = acc + cb_ref[:, cs]
    y = y_scr[...]
    mu = jnp.mean(y, axis=-1, keepdims=True)
    yc = y - mu
    var = jnp.mean(yc * yc, axis=-1, keepdims=True)
    z = yc * lax.rsqrt(var + EPS) * lg_ref[...] + lb_ref[...]
    o_ref[...] = (z * jax.nn.sigmoid(z)).astype(o_ref.dtype)


def conv_branch(rest, conv_w, conv_b, ln_g, ln_b, *, batch, seq, ch, bs):
    bs = min(bs, seq)
    assert CONV_HALO >= SUBLANES * (-(-CONV_WIDTH // SUBLANES)) and bs % CONV_ROW_CHUNK == 0
    rv = rest.reshape(batch, seq, rest.shape[1])
    nh = bs // CONV_HALO

    def cur(colblk):
        return pl.BlockSpec((None, bs, ch), lambda b, i: (b, i, colblk))

    def prev(colblk):
        return pl.BlockSpec((None, CONV_HALO, ch), lambda b, i: (b, jnp.maximum(i * nh - 1, 0), colblk))

    vec = pl.BlockSpec((1, ch), lambda b, i: (0, 0))
    out = pl.pallas_call(
        _conv_kernel,
        grid=(batch, seq // bs),
        in_specs=[cur(0), cur(1), prev(0), prev(1),
                  pl.BlockSpec((CONV_WIDTH, ch), lambda b, i: (0, 0)), vec, vec, vec],
        out_specs=pl.BlockSpec((None, bs, ch), lambda b, i: (b, i, 0)),
        out_shape=jax.ShapeDtypeStruct((batch, seq, ch), BF16),
        scratch_shapes=[pltpu.VMEM((CONV_HALO + bs, ch), F32), pltpu.VMEM((bs, ch), F32)],
        compiler_params=_params("arbitrary", "arbitrary"),
        name="conv_branch",
    )(rv, rv, rv, rv, conv_w, conv_b, ln_g, ln_b)
    return out.reshape(batch * seq, ch)


def _merge_kernel(o0_ref, o1_ref, o2_ref, l0_ref, l1_ref, l2_ref, u_ref, ga_ref, gc_ref,
                  wa_ref, wc_ref, ba_ref, bc_ref, out_ref, os_scr):
    @pl.when(pl.program_id(1) == 0)
    def _():
        l0, l1, l2 = l0_ref[...], l1_ref[...], l2_ref[...]
        m = jnp.maximum(jnp.maximum(l0, l1), l2)
        es = [jnp.exp(l0 - m), jnp.exp(l1 - m), jnp.exp(l2 - m)]
        inv = 1.0 / (es[0] + es[1] + es[2])
        for g, o_ref in enumerate((o0_ref, o1_ref, o2_ref)):
            alpha = es[g] * inv
            for h in range(HEADS_PER_GROUP):
                src = slice(h * HEAD_DIM, (h + 1) * HEAD_DIM)
                dst = slice((g * HEADS_PER_GROUP + h) * HEAD_DIM, (g * HEADS_PER_GROUP + h + 1) * HEAD_DIM)
                os_scr[:, dst] = (o_ref[:, src].astype(F32) * alpha[:, h:h + 1]).astype(os_scr.dtype)

    ya = jnp.dot(os_scr[...], wa_ref[...], preferred_element_type=F32)
    yc = jnp.dot(u_ref[...], wc_ref[...], preferred_element_type=F32)
    gate_a = jax.nn.sigmoid(ga_ref[...].astype(F32) + ba_ref[...])
    gate_c = jax.nn.sigmoid(gc_ref[...].astype(F32) + bc_ref[...])
    out_ref[...] = (gate_a * ya + gate_c * yc).astype(out_ref.dtype)


def gated_merge(outs, lses, u, rest, wa, wc, ba, bc, *, d, bm, bn):
    t = u.shape[0]
    ch = u.shape[1]
    bm, bn = min(bm, t), min(bn, d)
    ga_blk = (2 * ch) // bn
    gc_blk = (2 * ch + d) // bn
    row = lambda w: pl.BlockSpec((bm, w), lambda i, j: (i, 0))
    return pl.pallas_call(
        _merge_kernel,
        grid=(t // bm, d // bn),
        in_specs=[row(GROUP_WIDTH)] * 3 + [row(LANES)] * 3 + [
            row(ch),
            pl.BlockSpec((bm, bn), lambda i, j: (i, ga_blk + j)),
            pl.BlockSpec((bm, bn), lambda i, j: (i, gc_blk + j)),
            pl.BlockSpec((ATTN_WIDTH, bn), lambda i, j: (0, j)),
            pl.BlockSpec((ch, bn), lambda i, j: (0, j)),
            pl.BlockSpec((1, bn), lambda i, j: (0, j)),
            pl.BlockSpec((1, bn), lambda i, j: (0, j)),
        ],
        out_specs=pl.BlockSpec((bm, bn), lambda i, j: (i, j)),
        out_shape=jax.ShapeDtypeStruct((t, d), BF16),
        scratch_shapes=[pltpu.VMEM((bm, ATTN_WIDTH), BF16)],
        compiler_params=_params("arbitrary", "arbitrary"),
        name="gated_merge",
    )(*outs, *lses, u, rest, rest, wa, wc, ba, bc)


def _proj_residual_kernel(x_ref, a_ref, w_ref, o_ref):
    o_ref[...] = x_ref[...] + jnp.dot(a_ref[...], w_ref[...], preferred_element_type=F32)


def proj_residual(x, a, w, *, bm, bn):
    t, d = x.shape
    k = a.shape[1]
    bm, bn = min(bm, t), min(bn, d)
    return pl.pallas_call(
        _proj_residual_kernel,
        grid=(t // bm, d // bn),
        in_specs=[
            pl.BlockSpec((bm, bn), lambda i, j: (i, j)),
            pl.BlockSpec((bm, k), lambda i, j: (i, 0)),
            pl.BlockSpec((k, bn), lambda i, j: (0, j)),
        ],
        out_specs=pl.BlockSpec((bm, bn), lambda i, j: (i, j)),
        out_shape=jax.ShapeDtypeStruct((t, d), F32),
        compiler_params=_params("arbitrary", "arbitrary"),
        name="proj_residual",
    )(x, a, w)


def _swiglu_step(h, w1_ref, w3_ref, w2_ref):
    a = jnp.dot(h, w1_ref[...], preferred_element_type=F32)
    b = jnp.dot(h, w3_ref[...], preferred_element_type=F32)
    act = (a * jax.nn.sigmoid(a) * b).astype(BF16)
    return jnp.dot(act, w2_ref[...], preferred_element_type=F32)


def _ffn_kernel(x_ref, g_ref, w1_ref, w3_ref, w2_ref, o_ref, h_scr):
    @pl.when(pl.program_id(1) == 0)
    def _():
        _norm_rows_to(h_scr, x_ref, g_ref)
        o_ref[...] = x_ref[...]

    o_ref[...] += _swiglu_step(h_scr[...], w1_ref, w3_ref, w2_ref)


def dense_ffn(x, g, w1, w3, w2, *, bm, bf):
    t, d = x.shape
    f = w1.shape[1]
    bm, bf = min(bm, t), min(bf, f)
    return pl.pallas_call(
        _ffn_kernel,
        grid=(t // bm, f // bf),
        in_specs=[
            pl.BlockSpec((bm, d), lambda i, k: (i, 0)),
            pl.BlockSpec((1, d), lambda i, k: (0, 0)),
            pl.BlockSpec((d, bf), lambda i, k: (0, k)),
            pl.BlockSpec((d, bf), lambda i, k: (0, k)),
            pl.BlockSpec((bf, d), lambda i, k: (k, 0)),
        ],
        out_specs=pl.BlockSpec((bm, d), lambda i, k: (i, 0)),
        out_shape=jax.ShapeDtypeStruct((t, d), F32),
        scratch_shapes=[pltpu.VMEM((bm, d), BF16)],
        compiler_params=_params("arbitrary", "arbitrary"),
        name="dense_ffn",
    )(x, g, w1, w3, w2)


def _router_kernel(x_ref, g_ref, r_ref, tri_ref, hn_ref, dest_ref, gate_ref, cnt_ref, carry_scr, *, capacity):
    @pl.when(pl.program_id(0) == 0)
    def _():
        carry_scr[...] = jnp.zeros_like(carry_scr)

    h = _rms_rows(x_ref[...], g_ref[...])
    hn_ref[...] = h
    logits = jnp.dot(h, r_ref[...], preferred_element_type=F32, precision=lax.Precision.HIGHEST)
    lane = lax.broadcasted_iota(I32, logits.shape, 1)
    logits = jnp.where(lane < N_EXPERTS, logits, NEG_INF)
    m1 = jnp.max(logits, axis=1, keepdims=True)
    i1 = jnp.min(jnp.where(logits == m1, lane, LANES), axis=1, keepdims=True)
    hot1 = lane == i1
    rest = jnp.where(hot1, NEG_INF, logits)
    m2 = jnp.max(rest, axis=1, keepdims=True)
    i2 = jnp.min(jnp.where(rest == m2, lane, LANES), axis=1, keepdims=True)
    hot2 = lane == i2
    e = jnp.exp(m2 - m1)
    w1 = 1.0 / (1.0 + e)
    w2 = e * w1
    sel = jnp.where(hot1, 1.0, 0.0) + jnp.where(hot2, 1.0, 0.0)
    rank = carry_scr[...] + jnp.dot(tri_ref[...], sel.astype(BF16), preferred_element_type=F32)
    r1 = jnp.sum(jnp.where(hot1, rank, 0.0), axis=1, keepdims=True).astype(I32)
    r2 = jnp.sum(jnp.where(hot2, rank, 0.0), axis=1, keepdims=True).astype(I32)
    carry_scr[...] = carry_scr[...] + jnp.sum(sel, axis=0, keepdims=True)
    dest_ref[...] = jnp.where(lane == 0, i1 * capacity + r1, jnp.where(lane == 1, i2 * capacity + r2, 0))
    gate_ref[...] = jnp.where(lane == 0, w1, jnp.where(lane == 1, w2, 0.0))
    cnt_ref[...] = jnp.broadcast_to(carry_scr[...], cnt_ref.shape)


def moe_router(x, g, router, *, bm):
    t, d = x.shape
    bm = min(bm, t)
    r_pad = jnp.zeros((d, LANES), F32).at[:, :N_EXPERTS].set(router)
    tri = (lax.broadcasted_iota(I32, (bm, bm), 0) > lax.broadcasted_iota(I32, (bm, bm), 1)).astype(BF16)
    return pl.pallas_call(
        functools.partial(_router_kernel, capacity=t),
        grid=(t // bm,),
        in_specs=[
            pl.BlockSpec((bm, d), lambda i: (i, 0)),
            pl.BlockSpec((1, d), lambda i: (0, 0)),
            pl.BlockSpec((d, LANES), lambda i: (0, 0)),
            pl.BlockSpec((bm, bm), lambda i: (0, 0)),
        ],
        out_specs=[
            pl.BlockSpec((bm, d), lambda i: (i, 0)),
            pl.BlockSpec((bm, LANES), lambda i: (i, 0)),
            pl.BlockSpec((bm, LANES), lambda i: (i, 0)),
            pl.BlockSpec((8, LANES), lambda i: (0, 0)),
        ],
        out_shape=[
            jax.ShapeDtypeStruct((t, d), F32),
            jax.ShapeDtypeStruct((t, LANES), I32),
            jax.ShapeDtypeStruct((t, LANES), F32),
            jax.ShapeDtypeStruct((8, LANES), F32),
        ],
        scratch_shapes=[pltpu.VMEM((1, LANES), F32)],
        compiler_params=_params("arbitrary"),
        name="moe_router",
    )(x, g, r_pad, tri)


def _row_copy(src_ref, src_row, dst_ref, dst_row, sem):
    return pltpu.make_async_copy(src_ref.at[pl.ds(src_row, 1)], dst_ref.at[pl.ds(dst_row, 1)], sem)


def _dispatch_kernel(cnt_ref, d0_ref, d1_ref, hn_ref, xs_ref, zero_scr, sem, pad_sem, *, capacity, row_tile):
    bm = hn_ref.shape[0]

    def pad_copies(do):
        for e in range(N_EXPERTS):
            cnt = cnt_ref[e]
            pad = (-cnt) & (row_tile - 1)
            head = pad & (SUBLANES - 1)
            off = e * capacity + cnt
            for b in range(SUBLANES - 1):
                @pl.when(b < head)
                def _(off=off, b=b):
                    do(pltpu.make_async_copy(zero_scr.at[pl.ds(0, 1)], xs_ref.at[pl.ds(off + b, 1)], pad_sem))
            off = off + head
            size = SUBLANES
            while size < row_tile:
                @pl.when((pad & size) != 0)
                def _(off=off, size=size):
                    dst = xs_ref.at[pl.ds(pl.multiple_of(off, SUBLANES), size)]
                    do(pltpu.make_async_copy(zero_scr.at[pl.ds(0, size)], dst, pad_sem))
                off = off + (pad & size)
                size *= 2

    @pl.when(pl.program_id(0) == 0)
    def _():
        zero_scr[...] = jnp.zeros_like(zero_scr)
        pad_copies(lambda cp: cp.start())
        pad_copies(lambda cp: cp.wait())

    def issue(t, carry):
        _row_copy(hn_ref, t, xs_ref, d0_ref[0, 0, t], sem).start()
        _row_copy(hn_ref, t, xs_ref, d1_ref[0, 0, t], sem).start()
        return carry

    def drain(t, carry):
        _row_copy(hn_ref, 0, xs_ref, 0, sem).wait()
        _row_copy(hn_ref, 0, xs_ref, 0, sem).wait()
        return carry

    lax.fori_loop(0, bm, issue, 0)
    lax.fori_loop(0, bm, drain, 0)


def moe_dispatch(counts, dest0, dest1, hn, *, bm, row_tile):
    t, d = hn.shape
    bm = min(bm, t)
    nt = t // bm
    idx_spec = pl.BlockSpec((1, 1, bm), lambda i, cnt: (i, 0, 0), memory_space=pltpu.SMEM)
    return pl.pallas_call(
        functools.partial(_dispatch_kernel, capacity=t, row_tile=row_tile),
        grid_spec=pltpu.PrefetchScalarGridSpec(
            num_scalar_prefetch=1,
            grid=(nt,),
            in_specs=[idx_spec, idx_spec, pl.BlockSpec((bm, d), lambda i, cnt: (i, 0))],
            out_specs=pl.BlockSpec(memory_space=pl.ANY),
            scratch_shapes=[pltpu.VMEM((row_tile // 2, d), F32),
                            pltpu.SemaphoreType.DMA(()), pltpu.SemaphoreType.DMA(())],
        ),
        out_shape=jax.ShapeDtypeStruct((N_EXPERTS * t, d), F32),
        compiler_params=_params("arbitrary"),
        name="moe_dispatch",
    )(counts, dest0.reshape(nt, 1, bm), dest1.reshape(nt, 1, bm), hn)


def _moe_ffn_kernel(te_ref, tb_ref, tv_ref, x_ref, w1_ref, w3_ref, w2_ref, o_ref, h_scr):
    @pl.when(tv_ref[pl.program_id(0)] != 0)
    def _():
        @pl.when(pl.program_id(1) == 0)
        def _():
            h_scr[...] = x_ref[...].astype(h_scr.dtype)
            o_ref[...] = jnp.zeros_like(o_ref)

        o_ref[...] += _swiglu_step(h_scr[...], w1_ref, w3_ref, w2_ref)


def moe_grouped_ffn(tile_expert, tile_block, tile_valid, xs, w1, w3, w2, *, row_tile, bf):
    rows, d = xs.shape
    f = w1.shape[2]
    bf = min(bf, f)
    nf = f // bf
    n_steps = tile_expert.shape[0]

    def fk(s, k, tv):
        return jnp.where(tv[s] != 0, k, nf - 1)

    return pl.pallas_call(
        _moe_ffn_kernel,
        grid_spec=pltpu.PrefetchScalarGridSpec(
            num_scalar_prefetch=3,
            grid=(n_steps, nf),
            in_specs=[
                pl.BlockSpec((row_tile, d), lambda s, k, te, tb, tv: (tb[s], 0)),
                pl.BlockSpec((None, d, bf), lambda s, k, te, tb, tv: (te[s], 0, fk(s, k, tv))),
                pl.BlockSpec((None, d, bf), lambda s, k, te, tb, tv: (te[s], 0, fk(s, k, tv))),
                pl.BlockSpec((None, bf, d), lambda s, k, te, tb, tv: (te[s], fk(s, k, tv), 0)),
            ],
            out_specs=pl.BlockSpec((row_tile, d), lambda s, k, te, tb, tv: (tb[s], 0)),
            scratch_shapes=[pltpu.VMEM((row_tile, d), BF16)],
        ),
        out_shape=jax.ShapeDtypeStruct((rows, d), F32),
        compiler_params=_params("arbitrary", "arbitrary"),
        name="moe_grouped_ffn",
    )(tile_expert, tile_block, tile_valid, xs, w1, w3, w2)


def _combine_kernel(d0_ref, d1_ref, x_ref, gate_ref, ys_ref, o_ref, buf, sem):
    bm = x_ref.shape[0]

    def issue(t, carry):
        _row_copy(ys_ref, d0_ref[0, 0, t], buf.at[0], t, sem).start()
        _row_copy(ys_ref, d1_ref[0, 0, t], buf.at[1], t, sem).start()
        return carry

    def drain(t, carry):
        _row_copy(ys_ref, 0, buf.at[0], 0, sem).wait()
        _row_copy(ys_ref, 0, buf.at[1], 0, sem).wait()
        return carry

    lax.fori_loop(0, bm, issue, 0)
    lax.fori_loop(0, bm, drain, 0)
    gates = gate_ref[...]
    o_ref[...] = x_ref[...] + gates[:, 0:1] * buf[0] + gates[:, 1:2] * buf[1]


def moe_combine(dest0, dest1, x, gates, ys, *, bm):
    t, d = x.shape
    bm = min(bm, t)
    nt = t // bm
    idx_spec = pl.BlockSpec((1, 1, bm), lambda i: (i, 0, 0), memory_space=pltpu.SMEM)
    return pl.pallas_call(
        _combine_kernel,
        grid=(nt,),
        in_specs=[idx_spec, idx_spec,
                  pl.BlockSpec((bm, d), lambda i: (i, 0)),
                  pl.BlockSpec((bm, LANES), lambda i: (i, 0)),
                  pl.BlockSpec(memory_space=pl.ANY)],
        out_specs=pl.BlockSpec((bm, d), lambda i: (i, 0)),
        out_shape=jax.ShapeDtypeStruct((t, d), F32),
        scratch_shapes=[pltpu.VMEM((2, bm, d), F32), pltpu.SemaphoreType.DMA(())],
        compiler_params=_params("arbitrary"),
        name="moe_combine",
    )(dest0.reshape(nt, 1, bm), dest1.reshape(nt, 1, bm), x, gates, ys)


def _tile_tables(counts, *, capacity, row_tile, n_steps):
    tiles = (counts + row_tile - 1) // row_tile
    ends = jnp.cumsum(tiles)
    total = ends[-1]
    step = jnp.minimum(jnp.arange(n_steps, dtype=I32), total - 1)
    expert = jnp.sum((step[:, None] >= ends[None, :]).astype(I32), axis=1)
    block = expert * (capacity // row_tile) + step - (ends - tiles)[expert]
    valid = (jnp.arange(n_steps, dtype=I32) < total).astype(I32)
    return expert.astype(I32), block.astype(I32), valid


def moe_ffn(x, g, router, w1, w3, w2, *, bm, row_tile, bf):
    t, d = x.shape
    row_tile = min(row_tile, t)
    hn, dest, gates, cnt = moe_router(x, g, router, bm=bm)
    counts = cnt[0, :N_EXPERTS].astype(I32)
    dest0, dest1 = dest[:, 0], dest[:, 1]
    n_steps = (2 * t) // row_tile + N_EXPERTS
    tile_expert, tile_block, tile_valid = _tile_tables(counts, capacity=t, row_tile=row_tile, n_steps=n_steps)
    xs = moe_dispatch(counts, dest0, dest1, hn, bm=bm, row_tile=row_tile)
    ys = moe_grouped_ffn(tile_expert, tile_block, tile_valid, xs, w1, w3, w2, row_tile=row_tile, bf=bf)
    return moe_combine(dest0, dest1, x, gates, ys, bm=bm)


def kernel(x, norm_mix, w_in, q_norm, k_norm, w_attn_o, conv_w, conv_b, conv_ln_g, conv_ln_b, w_conv_o, gate_b,
           w_out, norm_ffn, ffn_w1, ffn_w3, ffn_w2, router, moe_w1, moe_w3, moe_w2):
    batch, seq, d = x.shape
    depth = w_in.shape[0]
    ch = conv_w.shape[-1]
    t = batch * seq
    dilations = tuple(dil for _, dil in ATTN_GROUPS)
    strided = tuple(dil for dil in dilations if dil > 1)
    tabs = [rope_table(seq, dil) for dil in dilations]
    row = lambda v: v.reshape(1, -1).astype(F32)
    xt = x.reshape(t, d)
    for l in range(depth):
        w_in_l = w_in[l]
        hs = norm_permute(xt, row(norm_mix[l]), batch=batch, seq=seq, dilations=strided, bm=512)
        h_by_dil = dict(zip((1,) + strided, hs))
        outs, lses = [], []
        for grp, dil in enumerate(dilations):
            cols = [slice(part * ATTN_WIDTH + grp * GROUP_WIDTH, part * ATTN_WIDTH + (grp + 1) * GROUP_WIDTH)
                    for part in range(3)]
            w_g = jnp.concatenate([w_in_l[:, c] for c in cols], axis=1).astype(BF16)
            qkv_g = qkv_proj(h_by_dil[dil], w_g, tabs[grp], row(q_norm[l]), row(k_norm[l]), seq=seq, bm=1024)
            o_g, lse_g = dilated_attention(qkv_g, group=grp, batch=batch, seq=seq, lq=2048 // max(dil, 4))
            outs.append(o_g)
            lses.append(lse_g)
        rest = matmul(h_by_dil[1], w_in_l[:, 3 * ATTN_WIDTH:].astype(BF16), bm=1024, bn=2048)
        u = conv_branch(rest, conv_w[l].reshape(CONV_WIDTH, ch), row(conv_b[l]), row(conv_ln_g[l]),
                        row(conv_ln_b[l]), batch=batch, seq=seq, ch=ch, bs=256)
        merged = gated_merge(outs, lses, u, rest, w_attn_o[l].astype(BF16), w_conv_o[l].astype(BF16),
                             row(gate_b[l, :d]), row(gate_b[l, d:]), d=d, bm=512, bn=1024)
        xt = proj_residual(xt, merged, w_out[l].astype(BF16), bm=1024, bn=1024)
        g_ffn = row(norm_ffn[l])
        i = l // 2
        if l % 2 == 0:
            xt = dense_ffn(xt, g_ffn, ffn_w1[i].astype(BF16), ffn_w3[i].astype(BF16), ffn_w2[i].astype(BF16),
                           bm=512, bf=1024)
        else:
            xt = moe_ffn(xt, g_ffn, router[i], moe_w1[i].astype(BF16), moe_w3[i].astype(BF16),
                         moe_w2[i].astype(BF16), bm=512, row_tile=512, bf=1024)
    return xt.reshape(batch, seq, d)
```

```python
import functools

import jax
import jax.numpy as jnp
from jax import lax
from jax.experimental import pallas as pl
from jax.experimental.pallas import tpu as pltpu

F32 = jnp.float32
BF16 = jnp.bfloat16
I32 = jnp.int32

HEAD_DIM = 128
ATTN_GROUPS = ((128, 1), (512, 4), (2048, 16))
HEADS_PER_GROUP = 4
GROUP_WIDTH = HEADS_PER_GROUP * HEAD_DIM
ATTN_WIDTH = len(ATTN_GROUPS) * GROUP_WIDTH
ROPE_DIM = HEAD_DIM // 4
ROPE_THETA = 500000.0
CONV_WIDTH = 31
N_EXPERTS = 8
BLOCK = 128
EPS = 1e-6
NEG_INF = -1e30

LANES = 128
SUBLANES = 8
CONV_HALO = 32
VMEM_LIMIT_BYTES = 56 * 1024 * 1024


def _params(*semantics):
    return pltpu.CompilerParams(dimension_semantics=semantics, vmem_limit_bytes=VMEM_LIMIT_BYTES)


def _rms_rows(x, g):
    return x * lax.rsqrt(jnp.mean(x * x, axis=-1, keepdims=True) + EPS) * g


def _norm_rows_to(h_ref, x_ref, g_ref, row_chunk=256):
    n = x_ref.shape[0]
    row_chunk = min(row_chunk, n)

    def body(c, carry):
        rows = pl.ds(pl.multiple_of(c * row_chunk, row_chunk), row_chunk)
        h_ref[rows, :] = _rms_rows(x_ref[rows, :], g_ref[...]).astype(h_ref.dtype)
        return carry

    lax.fori_loop(0, n // row_chunk, body, 0)


def _norm_permute_kernel(x_ref, g_ref, h_ref, *rest, dilations, row_chunk=256):
    perm_refs, scr = rest[:-1], rest[-1]
    bm, dm = x_ref.shape
    nl = dm // LANES
    row_chunk = min(row_chunk, bm)

    def body(c, carry):
        rows = pl.ds(pl.multiple_of(c * row_chunk, row_chunk), row_chunk)
        h = _rms_rows(x_ref[rows, :], g_ref[...])
        h_ref[rows, :] = h.astype(h_ref.dtype)
        for l in range(nl):
            scr[l, rows, :] = h[:, l * LANES:(l + 1) * LANES]
        return carry

    lax.fori_loop(0, bm // row_chunk, body, 0)
    for p_ref, dil in zip(perm_refs, dilations):
        n = bm // dil
        for r in range(dil):
            for l in range(nl):
                p_ref[r, :, l * LANES:(l + 1) * LANES] = scr[l, pl.ds(r, n, stride=dil), :].astype(p_ref.dtype)


def norm_permute(x, g, *, batch, seq, dilations, bm):
    t, d = x.shape
    bm = min(bm, seq)
    xv = x.reshape(batch, seq, d)
    outs = pl.pallas_call(
        functools.partial(_norm_permute_kernel, dilations=dilations),
        grid=(batch, seq // bm),
        in_specs=[pl.BlockSpec((None, bm, d), lambda b, i: (b, i, 0)),
                  pl.BlockSpec((1, d), lambda b, i: (0, 0))],
        out_specs=[pl.BlockSpec((None, bm, d), lambda b, i: (b, i, 0))] + [
            pl.BlockSpec((None, dil, bm // dil, d), lambda b, i: (b, 0, i, 0)) for dil in dilations],
        out_shape=[jax.ShapeDtypeStruct((batch, seq, d), BF16)] + [
            jax.ShapeDtypeStruct((batch, dil, seq // dil, d), BF16) for dil in dilations],
        scratch_shapes=[pltpu.VMEM((d // LANES, bm, LANES), F32)],
        compiler_params=_params("arbitrary", "arbitrary"),
        name="norm_permute",
    )(xv, g)
    return [o.reshape(t, d) for o in outs]


def _matmul_kernel(a_ref, w_ref, o_ref):
    o_ref[...] = jnp.dot(a_ref[...], w_ref[...], preferred_element_type=F32).astype(o_ref.dtype)


def matmul(a, w, *, bm, bn):
    t, k = a.shape
    n = w.shape[1]
    bm, bn = min(bm, t), min(bn, n)
    return pl.pallas_call(
        _matmul_kernel,
        grid=(t // bm, n // bn),
        in_specs=[pl.BlockSpec((bm, k), lambda i, j: (i, 0)),
                  pl.BlockSpec((k, bn), lambda i, j: (0, j))],
        out_specs=pl.BlockSpec((bm, bn), lambda i, j: (i, j)),
        out_shape=jax.ShapeDtypeStruct((t, n), BF16),
        compiler_params=_params("arbitrary", "arbitrary"),
        name="matmul",
    )(a, w)


def _norm_rope(t, g, tab):
    y = _rms_rows(t, g)
    half = ROPE_DIM // 2
    return (y * tab[:, :LANES]
            + pltpu.roll(y, LANES - half, 1) * tab[:, LANES:2 * LANES]
            + pltpu.roll(y, half, 1) * tab[:, 2 * LANES:])


def _qkv_epilogue(acc_ref, tab_ref, qg_ref, kg_ref, o_ref):
    tab = tab_ref[...]
    for h in range(HEADS_PER_GROUP):
        qs = slice(h * HEAD_DIM, (h + 1) * HEAD_DIM)
        ks = slice(GROUP_WIDTH + h * HEAD_DIM, GROUP_WIDTH + (h + 1) * HEAD_DIM)
        o_ref[:, qs] = (_norm_rope(acc_ref[:, qs], qg_ref[...], tab) * (HEAD_DIM ** -0.5)).astype(o_ref.dtype)
        o_ref[:, ks] = _norm_rope(acc_ref[:, ks], kg_ref[...], tab).astype(o_ref.dtype)
    o_ref[:, 2 * GROUP_WIDTH:] = acc_ref[:, 2 * GROUP_WIDTH:].astype(o_ref.dtype)


def _qkv_proj_kernel(h_ref, w_ref, tab_ref, qg_ref, kg_ref, o_ref, acc_even, acc_odd):
    i = pl.program_id(0)

    @pl.when(i == 0)
    def _():
        acc_odd[...] = jnp.zeros_like(acc_odd)

    def step(acc_new, acc_done):
        acc_new[...] = jnp.dot(h_ref[...], w_ref[...], preferred_element_type=F32)
        _qkv_epilogue(acc_done, tab_ref, qg_ref, kg_ref, o_ref)

    @pl.when(i % 2 == 0)
    def _():
        step(acc_even, acc_odd)

    @pl.when(i % 2 == 1)
    def _():
        step(acc_odd, acc_even)


def qkv_proj(h, w, tab, qg, kg, *, seq, bm):
    t, d = h.shape
    n = w.shape[1]
    bm = min(bm, seq)
    per_batch = seq // bm
    n_tiles = t // bm
    done = lambda i: jnp.maximum(i - 1, 0)
    return pl.pallas_call(
        _qkv_proj_kernel,
        grid=(n_tiles + 1,),
        in_specs=[pl.BlockSpec((bm, d), lambda i: (jnp.minimum(i, n_tiles - 1), 0)),
                  pl.BlockSpec((d, n), lambda i: (0, 0)),
                  pl.BlockSpec((bm, tab.shape[1]), lambda i: (done(i) % per_batch, 0)),
                  pl.BlockSpec((1, HEAD_DIM), lambda i: (0, 0)),
                  pl.BlockSpec((1, HEAD_DIM), lambda i: (0, 0))],
        out_specs=pl.BlockSpec((bm, n), lambda i: (done(i), 0)),
        out_shape=jax.ShapeDtypeStruct((t, n), BF16),
        scratch_shapes=[pltpu.VMEM((bm, n), F32), pltpu.VMEM((bm, n), F32)],
        compiler_params=_params("arbitrary"),
        name="qkv_proj",
    )(h, w, tab, qg, kg)


def rope_table(seq, dil):
    half = ROPE_DIM // 2
    inv = jnp.power(jnp.float32(ROPE_THETA), -jnp.arange(half, dtype=F32) * 2.0 / ROPE_DIM)
    pos = (jnp.arange(seq // dil, dtype=I32)[None, :] * dil + jnp.arange(dil, dtype=I32)[:, None]).reshape(seq)
    ang = pos.astype(F32)[:, None] * inv[None, :]
    cos, sin = jnp.cos(ang), jnp.sin(ang)
    ones = jnp.ones((seq, HEAD_DIM - ROPE_DIM), F32)
    zeros_tail = jnp.zeros((seq, HEAD_DIM - half), F32)
    c = jnp.concatenate([cos, cos, ones], axis=1)
    s_lo = jnp.concatenate([-sin, zeros_tail], axis=1)
    s_hi = jnp.concatenate([jnp.zeros((seq, half), F32), sin, jnp.zeros((seq, HEAD_DIM - ROPE_DIM), F32)], axis=1)
    return jnp.concatenate([c, s_lo, s_hi], axis=1)


ATTN_UNITS_PER_STAGE = 8


def _attn_kernel(q_ref, kc_ref, kp_ref, vc_ref, vp_ref, o_ref, lse_ref, o_scr, lse_scr):
    has_prev = pl.program_id(1) > 0
    dil, lq = q_ref.shape[0], q_ref.shape[1]
    nsub = lq // BLOCK
    row = lax.broadcasted_iota(I32, (BLOCK, BLOCK), 0)
    col = lax.broadcasted_iota(I32, (BLOCK, BLOCK), 1)
    cur_mask = col <= row
    prev_mask = col >= row
    nt = (((1,), (1,)), ((), ()))
    rows = lambda j: slice(j * BLOCK, (j + 1) * BLOCK)
    lanes = lambda h: slice(h * HEAD_DIM, (h + 1) * HEAD_DIM)

    def attend(units):
        n = range(len(units))
        q = [q_ref[r, rows(j), lanes(h)] for r, j, h in units]
        k_cur = [kc_ref[r, rows(j), lanes(h)] for r, j, h in units]
        v_cur = [vc_ref[r, rows(j), lanes(h)] for r, j, h in units]
        k_prev = [kp_ref[r, :, lanes(h)] if j == 0 else kc_ref[r, rows(j - 1), lanes(h)] for r, j, h in units]
        v_prev = [vp_ref[r, :, lanes(h)] if j == 0 else vc_ref[r, rows(j - 1), lanes(h)] for r, j, h in units]
        s_cur = [lax.dot_general(q[i], k_cur[i], nt, preferred_element_type=F32) for i in n]
        s_prev = [lax.dot_general(q[i], k_prev[i], nt, preferred_element_type=F32) for i in n]
        s_cur = [jnp.where(cur_mask, s, NEG_INF) for s in s_cur]
        s_prev = [jnp.where(prev_mask, s, NEG_INF) for s in s_prev]
        s_prev = [jnp.where(has_prev, s_prev[i], NEG_INF) if units[i][1] == 0 else s_prev[i] for i in n]
        m = [jnp.maximum(jnp.max(s_cur[i], axis=1, keepdims=True), jnp.max(s_prev[i], axis=1, keepdims=True))
             for i in n]
        p_cur = [jnp.exp(s_cur[i] - m[i]) for i in n]
        p_prev = [jnp.exp(s_prev[i] - m[i]) for i in n]
        l = [jnp.sum(p_cur[i], axis=1, keepdims=True) + jnp.sum(p_prev[i], axis=1, keepdims=True) for i in n]
        o = [jnp.dot(p_cur[i].astype(BF16), v_cur[i], preferred_element_type=F32)
             + jnp.dot(p_prev[i].astype(BF16), v_prev[i], preferred_element_type=F32) for i in n]
        lse = [m[i] + jnp.log(l[i]) for i in n]
        for i0 in range(0, len(units), HEADS_PER_GROUP):
            r, j, _ = units[i0]
            tok = pl.ds(j * BLOCK * dil + r, BLOCK, stride=dil)
            lse_tile = jnp.zeros((BLOCK, LANES), F32)
            for i in range(i0, i0 + HEADS_PER_GROUP):
                h = units[i][2]
                o_scr[h, tok, :] = o[i] / l[i]
                lse_tile = jnp.where(col == h, lse[i], lse_tile)
            lse_scr[0, tok, :] = lse_tile

    per_residue = [(j, h) for j in range(nsub) for h in range(HEADS_PER_GROUP)]
    n_res = max(1, ATTN_UNITS_PER_STAGE // len(per_residue))
    assert dil % n_res == 0

    def residues(it, carry):
        for u0 in range(0, n_res * len(per_residue), ATTN_UNITS_PER_STAGE):
            flat = [(it * n_res + rr, j, h) for rr in range(n_res) for j, h in per_residue]
            attend(flat[u0:u0 + ATTN_UNITS_PER_STAGE])
        return carry

    if dil == n_res:
        residues(0, 0)
    else:
        lax.fori_loop(0, dil // n_res, residues, 0)
    for h in range(HEADS_PER_GROUP):
        o_ref[:, h * HEAD_DIM:(h + 1) * HEAD_DIM] = o_scr[h].astype(o_ref.dtype)
    lse_ref[...] = lse_scr[0]


def dilated_attention(qkv, *, group, batch, seq, lq):
    window, dil = ATTN_GROUPS[group]
    assert window // dil == BLOCK
    n_strided = seq // dil
    assert seq % dil == 0 and n_strided % BLOCK == 0
    lq = min(n_strided, lq)
    sub = lq // BLOCK
    tile = lq * dil
    qv = qkv.reshape(batch, dil, n_strided, qkv.shape[1])

    def cur(colblk):
        return pl.BlockSpec((None, dil, lq, GROUP_WIDTH), lambda b, i: (b, 0, i, colblk))

    def prev(colblk):
        return pl.BlockSpec((None, dil, BLOCK, GROUP_WIDTH), lambda b, i: (b, 0, jnp.maximum(i * sub - 1, 0), colblk))

    t = batch * seq
    per_batch = seq // tile
    return pl.pallas_call(
        _attn_kernel,
        grid=(batch, per_batch),
        in_specs=[cur(0), cur(1), prev(1), cur(2), prev(2)],
        out_specs=[pl.BlockSpec((tile, GROUP_WIDTH), lambda b, i: (b * per_batch + i, 0)),
                   pl.BlockSpec((tile, LANES), lambda b, i: (b * per_batch + i, 0))],
        out_shape=[jax.ShapeDtypeStruct((t, GROUP_WIDTH), BF16), jax.ShapeDtypeStruct((t, LANES), F32)],
        scratch_shapes=[pltpu.VMEM((HEADS_PER_GROUP, tile, HEAD_DIM), F32), pltpu.VMEM((1, tile, LANES), F32)],
        compiler_params=_params("arbitrary", "arbitrary"),
        name=f"dilated_attention_g{group}",
    )(qv, qv, qv, qv, qv)


CONV_ROW_CHUNK = 64
CONV_LANE_CHUNK = 256


def _conv_kernel(vc_ref, gc_ref, vp_ref, gp_ref, w_ref, cb_ref, lg_ref, lb_ref, o_ref, u_scr, y_scr):
    bs, ch = vc_ref.shape
    halo = vp_ref.shape[0]
    u_prev = vp_ref[...].astype(F32) * jax.nn.sigmoid(gp_ref[...].astype(F32))
    u_scr[0:halo, :] = jnp.where(pl.program_id(1) > 0, u_prev, 0.0)
    u_scr[halo:, :] = vc_ref[...].astype(F32) * jax.nn.sigmoid(gc_ref[...].astype(F32))
    rc, lc = CONV_ROW_CHUNK, CONV_LANE_CHUNK
    for r0 in range(0, bs, rc):
        for c0 in range(0, ch, lc):
            cs = slice(c0, c0 + lc)
            acc = None
            for b in range(SUBLANES):
                part = None
                for a in range(-(-CONV_WIDTH // SUBLANES)):
                    k = SUBLANES * a + b
                    if k >= CONV_WIDTH:
                        continue
                    j = CONV_WIDTH - 1 - k
                    lo = halo + r0 - SUBLANES * (a + 1)
                    term = w_ref[j:j + 1, cs] * u_scr[lo:lo + rc + SUBLANES, cs]
                    part = term if part is None else part + term
                shifted = part[SUBLANES - b:SUBLANES - b + rc]
                acc = shifted if acc is None else acc + shifted
            y_scr[r0:r0 + rc, cs] = acc + cb_ref[:, cs]
    y = y_scr[...]
    mu = jnp.mean(y, axis=-1, keepdims=True)
    yc = y - mu
    var = jnp.mean(yc * yc, axis=-1, keepdims=True)
    z = yc * lax.rsqrt(var + EPS) * lg_ref[...] + lb_ref[...]
    o_ref[...] = (z * jax.nn.sigmoid(z)).astype(o_ref.dtype)


def conv_branch(rest, conv_w, conv_b, ln_g, ln_b, *, batch, seq, ch, bs):
    bs = min(bs, seq)
    assert CONV_HALO >= SUBLANES * (-(-CONV_WIDTH // SUBLANES)) and bs % CONV_ROW_CHUNK == 0
    rv = rest.reshape(batch, seq, rest.shape[1])
    nh = bs // CONV_HALO

    def cur(colblk):
        return pl.BlockSpec((None, bs, ch), lambda b, i: (b, i, colblk))

    def prev(colblk):
        return pl.BlockSpec((None, CONV_HALO, ch), lambda b, i: (b, jnp.maximum(i * nh - 1, 0), colblk))

    vec = pl.BlockSpec((1, ch), lambda b, i: (0, 0))
    out = pl.pallas_call(
        _conv_kernel,
        grid=(batch, seq // bs),
        in_specs=[cur(0), cur(1), prev(0), prev(1),
                  pl.BlockSpec((CONV_WIDTH, ch), lambda b, i: (0, 0)), vec, vec, vec],
        out_specs=pl.BlockSpec((None, bs, ch), lambda b, i: (b, i, 0)),
        out_shape=jax.ShapeDtypeStruct((batch, seq, ch), BF16),
        scratch_shapes=[pltpu.VMEM((CONV_HALO + bs, ch), F32), pltpu.VMEM((bs, ch), F32)],
        compiler_params=_params("arbitrary", "arbitrary"),
        name="conv_branch",
    )(rv, rv, rv, rv, conv_w, conv_b, ln_g, ln_b)
    return out.reshape(batch * seq, ch)


def _merge_kernel(o0_ref, o1_ref, o2_ref, l0_ref, l1_ref, l2_ref, u_ref, ga_ref, gc_ref,
                  wa_ref, wc_ref, ba_ref, bc_ref, out_ref, os_scr):
    @pl.when(pl.program_id(1) == 0)
    def _():
        l0, l1, l2 = l0_ref[...], l1_ref[...], l2_ref[...]
        m = jnp.maximum(jnp.maximum(l0, l1), l2)
        es = [jnp.exp(l0 - m), jnp.exp(l1 - m), jnp.exp(l2 - m)]
        inv = 1.0 / (es[0] + es[1] + es[2])
        for g, o_ref in enumerate((o0_ref, o1_ref, o2_ref)):
            alpha = es[g] * inv
            for h in range(HEADS_PER_GROUP):
                src = slice(h * HEAD_DIM, (h + 1) * HEAD_DIM)
                dst = slice((g * HEADS_PER_GROUP + h) * HEAD_DIM, (g * HEADS_PER_GROUP + h + 1) * HEAD_DIM)
                os_scr[:, dst] = (o_ref[:, src].astype(F32) * alpha[:, h:h + 1]).astype(os_scr.dtype)

    ya = jnp.dot(os_scr[...], wa_ref[...], preferred_element_type=F32)
    yc = jnp.dot(u_ref[...], wc_ref[...], preferred_element_type=F32)
    gate_a = jax.nn.sigmoid(ga_ref[...].astype(F32) + ba_ref[...])
    gate_c = jax.nn.sigmoid(gc_ref[...].astype(F32) + bc_ref[...])
    out_ref[...] = (gate_a * ya + gate_c * yc).astype(out_ref.dtype)


def gated_merge(outs, lses, u, rest, wa, wc, ba, bc, *, d, bm, bn):
    t = u.shape[0]
    ch = u.shape[1]
    bm, bn = min(bm, t), min(bn, d)
    ga_blk = (2 * ch) // bn
    gc_blk = (2 * ch + d) // bn
    row = lambda w: pl.BlockSpec((bm, w), lambda i, j: (i, 0))
    return pl.pallas_call(
        _merge_kernel,
        grid=(t // bm, d // bn),
        in_specs=[row(GROUP_WIDTH)] * 3 + [row(LANES)] * 3 + [
            row(ch),
            pl.BlockSpec((bm, bn), lambda i, j: (i, ga_blk + j)),
            pl.BlockSpec((bm, bn), lambda i, j: (i, gc_blk + j)),
            pl.BlockSpec((ATTN_WIDTH, bn), lambda i, j: (0, j)),
            pl.BlockSpec((ch, bn), lambda i, j: (0, j)),
            pl.BlockSpec((1, bn), lambda i, j: (0, j)),
            pl.BlockSpec((1, bn), lambda i, j: (0, j)),
        ],
        out_specs=pl.BlockSpec((bm, bn), lambda i, j: (i, j)),
        out_shape=jax.ShapeDtypeStruct((t, d), BF16),
        scratch_shapes=[pltpu.VMEM((bm, ATTN_WIDTH), BF16)],
        compiler_params=_params("arbitrary", "arbitrary"),
        name="gated_merge",
    )(*outs, *lses, u, rest, rest, wa, wc, ba, bc)


def _proj_residual_kernel(x_ref, a_ref, w_ref, o_ref):
    o_ref[...] = x_ref[...] + jnp.dot(a_ref[...], w_ref[...], preferred_element_type=F32)


def proj_residual(x, a, w, *, bm, bn):
    t, d = x.shape
    k = a.shape[1]
    bm, bn = min(bm, t), min(bn, d)
    return pl.pallas_call(
        _proj_residual_kernel,
        grid=(t // bm, d // bn),
        in_specs=[
            pl.BlockSpec((bm, bn), lambda i, j: (i, j)),
            pl.BlockSpec((bm, k), lambda i, j: (i, 0)),
            pl.BlockSpec((k, bn), lambda i, j: (0, j)),
        ],
        out_specs=pl.BlockSpec((bm, bn), lambda i, j: (i, j)),
        out_shape=jax.ShapeDtypeStruct((t, d), F32),
        compiler_params=_params("arbitrary", "arbitrary"),
        name="proj_residual",
    )(x, a, w)


def _swiglu_step(h, w1_ref, w3_ref, w2_ref):
    a = jnp.dot(h, w1_ref[...], preferred_element_type=F32)
    b = jnp.dot(h, w3_ref[...], preferred_element_type=F32)
    act = (a * jax.nn.sigmoid(a) * b).astype(BF16)
    return jnp.dot(act, w2_ref[...], preferred_element_type=F32)


def _ffn_kernel(x_ref, g_ref, w1_ref, w3_ref, w2_ref, o_ref, h_scr):
    @pl.when(pl.program_id(1) == 0)
    def _():
        _norm_rows_to(h_scr, x_ref, g_ref)
        o_ref[...] = x_ref[...]

    o_ref[...] += _swiglu_step(h_scr[...], w1_ref, w3_ref, w2_ref)


def dense_ffn(x, g, w1, w3, w2, *, bm, bf):
    t, d = x.shape
    f = w1.shape[1]
    bm, bf = min(bm, t), min(bf, f)
    return pl.pallas_call(
        _ffn_kernel,
        grid=(t // bm, f // bf),
        in_specs=[
            pl.BlockSpec((bm, d), lambda i, k: (i, 0)),
            pl.BlockSpec((1, d), lambda i, k: (0, 0)),
            pl.BlockSpec((d, bf), lambda i, k: (0, k)),
            pl.BlockSpec((d, bf), lambda i, k: (0, k)),
            pl.BlockSpec((bf, d), lambda i, k: (k, 0)),
        ],
        out_specs=pl.BlockSpec((bm, d), lambda i, k: (i, 0)),
        out_shape=jax.ShapeDtypeStruct((t, d), F32),
        scratch_shapes=[pltpu.VMEM((bm, d), BF16)],
        compiler_params=_params("arbitrary", "arbitrary"),
        name="dense_ffn",
    )(x, g, w1, w3, w2)


def _router_kernel(x_ref, g_ref, r_ref, tri_ref, hn_ref, dest_ref, gate_ref, cnt_ref, carry_scr, *, capacity):
    @pl.when(pl.program_id(0) == 0)
    def _():
        carry_scr[...] = jnp.zeros_like(carry_scr)

    h = _rms_rows(x_ref[...], g_ref[...])
    hn_ref[...] = h
    logits = jnp.dot(h, r_ref[...], preferred_element_type=F32, precision=lax.Precision.HIGHEST)
    lane = lax.broadcasted_iota(I32, logits.shape, 1)
    logits = jnp.where(lane < N_EXPERTS, logits, NEG_INF)
    m1 = jnp.max(logits, axis=1, keepdims=True)
    i1 = jnp.min(jnp.where(logits == m1, lane, LANES), axis=1, keepdims=True)
    hot1 = lane == i1
    rest = jnp.where(hot1, NEG_INF, logits)
    m2 = jnp.max(rest, axis=1, keepdims=True)
    i2 = jnp.min(jnp.where(rest == m2, lane, LANES), axis=1, keepdims=True)
    hot2 = lane == i2
    e = jnp.exp(m2 - m1)
    w1 = 1.0 / (1.0 + e)
    w2 = e * w1
    sel = jnp.where(hot1, 1.0, 0.0) + jnp.where(hot2, 1.0, 0.0)
    rank = carry_scr[...] + jnp.dot(tri_ref[...], sel.astype(BF16), preferred_element_type=F32)
    r1 = jnp.sum(jnp.where(hot1, rank, 0.0), axis=1, keepdims=True).astype(I32)
    r2 = jnp.sum(jnp.where(hot2, rank, 0.0), axis=1, keepdims=True).astype(I32)
    carry_scr[...] = carry_scr[...] + jnp.sum(sel, axis=0, keepdims=True)
    dest_ref[...] = jnp.where(lane == 0, i1 * capacity + r1, jnp.where(lane == 1, i2 * capacity + r2, 0))
    gate_ref[...] = jnp.where(lane == 0, w1, jnp.where(lane == 1, w2, 0.0))
    cnt_ref[...] = jnp.broadcast_to(carry_scr[...], cnt_ref.shape)


def moe_router(x, g, router, *, bm):
    t, d = x.shape
    bm = min(bm, t)
    r_pad = jnp.zeros((d, LANES), F32).at[:, :N_EXPERTS].set(router)
    tri = (lax.broadcasted_iota(I32, (bm, bm), 0) > lax.broadcasted_iota(I32, (bm, bm), 1)).astype(BF16)
    return pl.pallas_call(
        functools.partial(_router_kernel, capacity=t),
        grid=(t // bm,),
        in_specs=[
            pl.BlockSpec((bm, d), lambda i: (i, 0)),
            pl.BlockSpec((1, d), lambda i: (0, 0)),
            pl.BlockSpec((d, LANES), lambda i: (0, 0)),
            pl.BlockSpec((bm, bm), lambda i: (0, 0)),
        ],
        out_specs=[
            pl.BlockSpec((bm, d), lambda i: (i, 0)),
            pl.BlockSpec((bm, LANES), lambda i: (i, 0)),
            pl.BlockSpec((bm, LANES), lambda i: (i, 0)),
            pl.BlockSpec((8, LANES), lambda i: (0, 0)),
        ],
        out_shape=[
            jax.ShapeDtypeStruct((t, d), F32),
            jax.ShapeDtypeStruct((t, LANES), I32),
            jax.ShapeDtypeStruct((t, LANES), F32),
            jax.ShapeDtypeStruct((8, LANES), F32),
        ],
        scratch_shapes=[pltpu.VMEM((1, LANES), F32)],
        compiler_params=_params("arbitrary"),
        name="moe_router",
    )(x, g, r_pad, tri)


def _row_copy(src_ref, src_row, dst_ref, dst_row, sem):
    return pltpu.make_async_copy(src_ref.at[pl.ds(src_row, 1)], dst_ref.at[pl.ds(dst_row, 1)], sem)


def _dispatch_kernel(cnt_ref, d0_ref, d1_ref, hn_ref, xs_ref, zero_scr, sem, pad_sem, *, capacity, row_tile):
    bm = hn_ref.shape[0]

    def pad_copies(do):
        for e in range(N_EXPERTS):
            cnt = cnt_ref[e]
            pad = (-cnt) & (row_tile - 1)
            head = pad & (SUBLANES - 1)
            off = e * capacity + cnt
            for b in range(SUBLANES - 1):
                @pl.when(b < head)
                def _(off=off, b=b):
                    do(pltpu.make_async_copy(zero_scr.at[pl.ds(0, 1)], xs_ref.at[pl.ds(off + b, 1)], pad_sem))
            off = off + head
            size = SUBLANES
            while size < row_tile:
                @pl.when((pad & size) != 0)
                def _(off=off, size=size):
                    dst = xs_ref.at[pl.ds(pl.multiple_of(off, SUBLANES), size)]
                    do(pltpu.make_async_copy(zero_scr.at[pl.ds(0, size)], dst, pad_sem))
                off = off + (pad & size)
                size *= 2

    @pl.when(pl.program_id(0) == 0)
    def _():
        zero_scr[...] = jnp.zeros_like(zero_scr)
        pad_copies(lambda cp: cp.start())
        pad_copies(lambda cp: cp.wait())

    def issue(t, carry):
        _row_copy(hn_ref, t, xs_ref, d0_ref[0, 0, t], sem).start()
        _row_copy(hn_ref, t, xs_ref, d1_ref[0, 0, t], sem).start()
        return carry

    def drain(t, carry):
        _row_copy(hn_ref, 0, xs_ref, 0, sem).wait()
        _row_copy(hn_ref, 0, xs_ref, 0, sem).wait()
        return carry

    lax.fori_loop(0, bm, issue, 0)
    lax.fori_loop(0, bm, drain, 0)


def moe_dispatch(counts, dest0, dest1, hn, *, bm, row_tile):
    t, d = hn.shape
    bm = min(bm, t)
    nt = t // bm
    idx_spec = pl.BlockSpec((1, 1, bm), lambda i, cnt: (i, 0, 0), memory_space=pltpu.SMEM)
    return pl.pallas_call(
        functools.partial(_dispatch_kernel, capacity=t, row_tile=row_tile),
        grid_spec=pltpu.PrefetchScalarGridSpec(
            num_scalar_prefetch=1,
            grid=(nt,),
            in_specs=[idx_spec, idx_spec, pl.BlockSpec((bm, d), lambda i, cnt: (i, 0))],
            out_specs=pl.BlockSpec(memory_space=pl.ANY),
            scratch_shapes=[pltpu.VMEM((row_tile // 2, d), F32),
                            pltpu.SemaphoreType.DMA(()), pltpu.SemaphoreType.DMA(())],
        ),
        out_shape=jax.ShapeDtypeStruct((N_EXPERTS * t, d), F32),
        compiler_params=_params("arbitrary"),
        name="moe_dispatch",
    )(counts, dest0.reshape(nt, 1, bm), dest1.reshape(nt, 1, bm), hn)


def _moe_ffn_kernel(te_ref, tb_ref, tv_ref, x_ref, w1_ref, w3_ref, w2_ref, o_ref, h_scr):
    @pl.when(tv_ref[pl.program_id(0)] != 0)
    def _():
        @pl.when(pl.program_id(1) == 0)
        def _():
            h_scr[...] = x_ref[...].astype(h_scr.dtype)
            o_ref[...] = jnp.zeros_like(o_ref)

        o_ref[...] += _swiglu_step(h_scr[...], w1_ref, w3_ref, w2_ref)


def moe_grouped_ffn(tile_expert, tile_block, tile_valid, xs, w1, w3, w2, *, row_tile, bf):
    rows, d = xs.shape
    f = w1.shape[2]
    bf = min(bf, f)
    nf = f // bf
    n_steps = tile_expert.shape[0]

    def fk(s, k, tv):
        return jnp.where(tv[s] != 0, k, nf - 1)

    return pl.pallas_call(
        _moe_ffn_kernel,
        grid_spec=pltpu.PrefetchScalarGridSpec(
            num_scalar_prefetch=3,
            grid=(n_steps, nf),
            in_specs=[
                pl.BlockSpec((row_tile, d), lambda s, k, te, tb, tv: (tb[s], 0)),
                pl.BlockSpec((None, d, bf), lambda s, k, te, tb, tv: (te[s], 0, fk(s, k, tv))),
                pl.BlockSpec((None, d, bf), lambda s, k, te, tb, tv: (te[s], 0, fk(s, k, tv))),
                pl.BlockSpec((None, bf, d), lambda s, k, te, tb, tv: (te[s], fk(s, k, tv), 0)),
            ],
            out_specs=pl.BlockSpec((row_tile, d), lambda s, k, te, tb, tv: (tb[s], 0)),
            scratch_shapes=[pltpu.VMEM((row_tile, d), BF16)],
        ),
        out_shape=jax.ShapeDtypeStruct((rows, d), F32),
        compiler_params=_params("arbitrary", "arbitrary"),
        name="moe_grouped_ffn",
    )(tile_expert, tile_block, tile_valid, xs, w1, w3, w2)


def _combine_kernel(d0_ref, d1_ref, x_ref, gate_ref, ys_ref, o_ref, buf, sem):
    bm = x_ref.shape[0]

    def issue(t, carry):
        _row_copy(ys_ref, d0_ref[0, 0, t], buf.at[0], t, sem).start()
        _row_copy(ys_ref, d1_ref[0, 0, t], buf.at[1], t, sem).start()
        return carry

    def drain(t, carry):
        _row_copy(ys_ref, 0, buf.at[0], 0, sem).wait()
        _row_copy(ys_ref, 0, buf.at[1], 0, sem).wait()
        return carry

    lax.fori_loop(0, bm, issue, 0)
    lax.fori_loop(0, bm, drain, 0)
    gates = gate_ref[...]
    o_ref[...] = x_ref[...] + gates[:, 0:1] * buf[0] + gates[:, 1:2] * buf[1]


def moe_combine(dest0, dest1, x, gates, ys, *, bm):
    t, d = x.shape
    bm = min(bm, t)
    nt = t // bm
    idx_spec = pl.BlockSpec((1, 1, bm), lambda i: (i, 0, 0), memory_space=pltpu.SMEM)
    return pl.pallas_call(
        _combine_kernel,
        grid=(nt,),
        in_specs=[idx_spec, idx_spec,
                  pl.BlockSpec((bm, d), lambda i: (i, 0)),
                  pl.BlockSpec((bm, LANES), lambda i: (i, 0)),
                  pl.BlockSpec(memory_space=pl.ANY)],
        out_specs=pl.BlockSpec((bm, d), lambda i: (i, 0)),
        out_shape=jax.ShapeDtypeStruct((t, d), F32),
        scratch_shapes=[pltpu.VMEM((2, bm, d), F32), pltpu.SemaphoreType.DMA(())],
        compiler_params=_params("arbitrary"),
        name="moe_combine",
    )(dest0.reshape(nt, 1, bm), dest1.reshape(nt, 1, bm), x, gates, ys)


def _tile_tables(counts, *, capacity, row_tile, n_steps):
    tiles = (counts + row_tile - 1) // row_tile
    ends = jnp.cumsum(tiles)
    total = ends[-1]
    step = jnp.minimum(jnp.arange(n_steps, dtype=I32), total - 1)
    expert = jnp.sum((step[:, None] >= ends[None, :]).astype(I32), axis=1)
    block = expert * (capacity // row_tile) + step - (ends - tiles)[expert]
    valid = (jnp.arange(n_steps, dtype=I32) < total).astype(I32)
    return expert.astype(I32), block.astype(I32), valid


def moe_ffn(x, g, router, w1, w3, w2, *, bm, row_tile, bf):
    t, d = x.shape
    row_tile = min(row_tile, t)
    hn, dest, gates, cnt = moe_router(x, g, router, bm=bm)
    counts = cnt[0, :N_EXPERTS].astype(I32)
    dest0, dest1 = dest[:, 0], dest[:, 1]
    n_steps = (2 * t) // row_tile + N_EXPERTS
    tile_expert, tile_block, tile_valid = _tile_tables(counts, capacity=t, row_tile=row_tile, n_steps=n_steps)
    xs = moe_dispatch(counts, dest0, dest1, hn, bm=bm, row_tile=row_tile)
    ys = moe_grouped_ffn(tile_expert, tile_block, tile_valid, xs, w1, w3, w2, row_tile=row_tile, bf=bf)
    return moe_combine(dest0, dest1, x, gates, ys, bm=bm)


def kernel(x, norm_mix, w_in, q_norm, k_norm, w_attn_o, conv_w, conv_b, conv_ln_g, conv_ln_b, w_conv_o, gate_b,
           w_out, norm_ffn, ffn_w1, ffn_w3, ffn_w2, router, moe_w1, moe_w3, moe_w2):
    batch, seq, d = x.shape
    depth = w_in.shape[0]
    ch = conv_w.shape[-1]
    t = batch * seq
    dilations = tuple(dil for _, dil in ATTN_GROUPS)
    strided = tuple(dil for dil in dilations if dil > 1)
    tabs = [rope_table(seq, dil) for dil in dilations]
    row = lambda v: v.reshape(1, -1).astype(F32)
    xt = x.reshape(t, d)
    for l in range(depth):
        w_in_l = w_in[l]
        hs = norm_permute(xt, row(norm_mix[l]), batch=batch, seq=seq, dilations=strided, bm=512)
        h_by_dil = dict(zip((1,) + strided, hs))
        outs, lses = [], []
        for grp, dil in enumerate(dilations):
            cols = [slice(part * ATTN_WIDTH + grp * GROUP_WIDTH, part * ATTN_WIDTH + (grp + 1) * GROUP_WIDTH)
                    for part in range(3)]
            w_g = jnp.concatenate([w_in_l[:, c] for c in cols], axis=1).astype(BF16)
            qkv_g = qkv_proj(h_by_dil[dil], w_g, tabs[grp], row(q_norm[l]), row(k_norm[l]), seq=seq, bm=1024)
            o_g, lse_g = dilated_attention(qkv_g, group=grp, batch=batch, seq=seq, lq=2048 // max(dil, 4))
            outs.append(o_g)
            lses.append(lse_g)
        rest = matmul(h_by_dil[1], w_in_l[:, 3 * ATTN_WIDTH:].astype(BF16), bm=1024, bn=2048)
        u = conv_branch(rest, conv_w[l].reshape(CONV_WIDTH, ch), row(conv_b[l]), row(conv_ln_g[l]),
                        row(conv_ln_b[l]), batch=batch, seq=seq, ch=ch, bs=256)
        merged = gated_merge(outs, lses, u, rest, w_attn_o[l].astype(BF16), w_conv_o[l].astype(BF16),
                             row(gate_b[l, :d]), row(gate_b[l, d:]), d=d, bm=512, bn=1024)
        xt = proj_residual(xt, merged, w_out[l].astype(BF16), bm=1024, bn=1024)
        g_ffn = row(norm_ffn[l])
        i = l // 2
        if l % 2 == 0:
            xt = dense_ffn(xt, g_ffn, ffn_w1[i].astype(BF16), ffn_w3[i].astype(BF16), ffn_w2[i].astype(BF16),
                           bm=512, bf=1024)
        else:
            xt = moe_ffn(xt, g_ffn, router[i], moe_w1[i].astype(BF16), moe_w3[i].astype(BF16),
                         moe_w2[i].astype(BF16), bm=512, row_tile=512, bf=1024)
    return xt.reshape(batch, seq, d)
```

```python
import functools

import jax
import jax.numpy as jnp
from jax import lax
from jax.experimental import pallas as pl
from jax.experimental.pallas import tpu as pltpu

F32 = jnp.float32
BF16 = jnp.bfloat16
I32 = jnp.int32

HEAD_DIM = 128
ATTN_GROUPS = ((128, 1), (512, 4), (2048, 16))
HEADS_PER_GROUP = 4
GROUP_WIDTH = HEADS_PER_GROUP * HEAD_DIM
ATTN_WIDTH = len(ATTN_GROUPS) * GROUP_WIDTH
ROPE_DIM = HEAD_DIM // 4
ROPE_THETA = 500000.0
CONV_WIDTH = 31
N_EXPERTS = 8
BLOCK = 128
EPS = 1e-6
NEG_INF = -1e30

LANES = 128
SUBLANES = 8
CONV_HALO = 32
VMEM_LIMIT_BYTES = 56 * 1024 * 1024


def _params(*semantics):
    return pltpu.CompilerParams(dimension_semantics=semantics, vmem_limit_bytes=VMEM_LIMIT_BYTES)


def _rms_rows(x, g):
    return x * lax.rsqrt(jnp.mean(x * x, axis=-1, keepdims=True) + EPS) * g


def _norm_rows_to(h_ref, x_ref, g_ref, row_chunk=256):
    n = x_ref.shape[0]
    row_chunk = min(row_chunk, n)

    def body(c, carry):
        rows = pl.ds(pl.multiple_of(c * row_chunk, row_chunk), row_chunk)
        h_ref[rows, :] = _rms_rows(x_ref[rows, :], g_ref[...]).astype(h_ref.dtype)
        return carry

    lax.fori_loop(0, n // row_chunk, body, 0)


def _norm_permute_kernel(x_ref, g_ref, h_ref, *rest, dilations, row_chunk=256):
    perm_refs, scr = rest[:-1], rest[-1]
    bm, dm = x_ref.shape
    nl = dm // LANES
    row_chunk = min(row_chunk, bm)

    def body(c, carry):
        rows = pl.ds(pl.multiple_of(c * row_chunk, row_chunk), row_chunk)
        h = _rms_rows(x_ref[rows, :], g_ref[...])
        h_ref[rows, :] = h.astype(h_ref.dtype)
        for l in range(nl):
            scr[l, rows, :] = h[:, l * LANES:(l + 1) * LANES]
        return carry

    lax.fori_loop(0, bm // row_chunk, body, 0)
    for p_ref, dil in zip(perm_refs, dilations):
        n = bm // dil
        for r in range(dil):
            for l in range(nl):
                p_ref[r, :, l * LANES:(l + 1) * LANES] = scr[l, pl.ds(r, n, stride=dil), :].astype(p_ref.dtype)


def norm_permute(x, g, *, batch, seq, dilations, bm):
    t, d = x.shape
    bm = min(bm, seq)
    xv = x.reshape(batch, seq, d)
    outs = pl.pallas_call(
        functools.partial(_norm_permute_kernel, dilations=dilations),
        grid=(batch, seq // bm),
        in_specs=[pl.BlockSpec((None, bm, d), lambda b, i: (b, i, 0)),
                  pl.BlockSpec((1, d), lambda b, i: (0, 0))],
        out_specs=[pl.BlockSpec((None, bm, d), lambda b, i: (b, i, 0))] + [
            pl.BlockSpec((None, dil, bm // dil, d), lambda b, i: (b, 0, i, 0)) for dil in dilations],
        out_shape=[jax.ShapeDtypeStruct((batch, seq, d), BF16)] + [
            jax.ShapeDtypeStruct((batch, dil, seq // dil, d), BF16) for dil in dilations],
        scratch_shapes=[pltpu.VMEM((d // LANES, bm, LANES), F32)],
        compiler_params=_params("arbitrary", "arbitrary"),
        name="norm_permute",
    )(xv, g)
    return [o.reshape(t, d) for o in outs]


def _matmul_kernel(a_ref, w_ref, o_ref):
    o_ref[...] = jnp.dot(a_ref[...], w_ref[...], preferred_element_type=F32).astype(o_ref.dtype)


def matmul(a, w, *, bm, bn):
    t, k = a.shape
    n = w.shape[1]
    bm, bn = min(bm, t), min(bn, n)
    return pl.pallas_call(
        _matmul_kernel,
        grid=(t // bm, n // bn),
        in_specs=[pl.BlockSpec((bm, k), lambda i, j: (i, 0)),
                  pl.BlockSpec((k, bn), lambda i, j: (0, j))],
        out_specs=pl.BlockSpec((bm, bn), lambda i, j: (i, j)),
        out_shape=jax.ShapeDtypeStruct((t, n), BF16),
        compiler_params=_params("arbitrary", "arbitrary"),
        name="matmul",
    )(a, w)


def _norm_rope(t, g, tab):
    y = _rms_rows(t, g)
    half = ROPE_DIM // 2
    return (y * tab[:, :LANES]
            + pltpu.roll(y, LANES - half, 1) * tab[:, LANES:2 * LANES]
            + pltpu.roll(y, half, 1) * tab[:, 2 * LANES:])


def _qkv_epilogue(acc_ref, tab_ref, qg_ref, kg_ref, o_ref):
    tab = tab_ref[...]
    for h in range(HEADS_PER_GROUP):
        qs = slice(h * HEAD_DIM, (h + 1) * HEAD_DIM)
        ks = slice(GROUP_WIDTH + h * HEAD_DIM, GROUP_WIDTH + (h + 1) * HEAD_DIM)
        o_ref[:, qs] = (_norm_rope(acc_ref[:, qs], qg_ref[...], tab) * (HEAD_DIM ** -0.5)).astype(o_ref.dtype)
        o_ref[:, ks] = _norm_rope(acc_ref[:, ks], kg_ref[...], tab).astype(o_ref.dtype)
    o_ref[:, 2 * GROUP_WIDTH:] = acc_ref[:, 2 * GROUP_WIDTH:].astype(o_ref.dtype)


def _qkv_proj_kernel(h_ref, w_ref, tab_ref, qg_ref, kg_ref, o_ref, acc_even, acc_odd):
    i = pl.program_id(0)

    @pl.when(i == 0)
    def _():
        acc_odd[...] = jnp.zeros_like(acc_odd)

    def step(acc_new, acc_done):
        acc_new[...] = jnp.dot(h_ref[...], w_ref[...], preferred_element_type=F32)
        _qkv_epilogue(acc_done, tab_ref, qg_ref, kg_ref, o_ref)

    @pl.when(i % 2 == 0)
    def _():
        step(acc_even, acc_odd)

    @pl.when(i % 2 == 1)
    def _():
        step(acc_odd, acc_even)


def qkv_proj(h, w, tab, qg, kg, *, seq, bm):
    t, d = h.shape
    n = w.shape[1]
    bm = min(bm, seq)
    per_batch = seq // bm
    n_tiles = t // bm
    done = lambda i: jnp.maximum(i - 1, 0)
    return pl.pallas_call(
        _qkv_proj_kernel,
        grid=(n_tiles + 1,),
        in_specs=[pl.BlockSpec((bm, d), lambda i: (jnp.minimum(i, n_tiles - 1), 0)),
                  pl.BlockSpec((d, n), lambda i: (0, 0)),
                  pl.BlockSpec((bm, tab.shape[1]), lambda i: (done(i) % per_batch, 0)),
                  pl.BlockSpec((1, HEAD_DIM), lambda i: (0, 0)),
                  pl.BlockSpec((1, HEAD_DIM), lambda i: (0, 0))],
        out_specs=pl.BlockSpec((bm, n), lambda i: (done(i), 0)),
        out_shape=jax.ShapeDtypeStruct((t, n), BF16),
        scratch_shapes=[pltpu.VMEM((bm, n), F32), pltpu.VMEM((bm, n), F32)],
        compiler_params=_params("arbitrary"),
        name="qkv_proj",
    )(h, w, tab, qg, kg)


def rope_table(seq, dil):
    half = ROPE_DIM // 2
    inv = jnp.power(jnp.float32(ROPE_THETA), -jnp.arange(half, dtype=F32) * 2.0 / ROPE_DIM)
    pos = (jnp.arange(seq // dil, dtype=I32)[None, :] * dil + jnp.arange(dil, dtype=I32)[:, None]).reshape(seq)
    ang = pos.astype(F32)[:, None] * inv[None, :]
    cos, sin = jnp.cos(ang), jnp.sin(ang)
    ones = jnp.ones((seq, HEAD_DIM - ROPE_DIM), F32)
    zeros_tail = jnp.zeros((seq, HEAD_DIM - half), F32)
    c = jnp.concatenate([cos, cos, ones], axis=1)
    s_lo = jnp.concatenate([-sin, zeros_tail], axis=1)
    s_hi = jnp.concatenate([jnp.zeros((seq, half), F32), sin, jnp.zeros((seq, HEAD_DIM - ROPE_DIM), F32)], axis=1)
    return jnp.concatenate([c, s_lo, s_hi], axis=1)


ATTN_UNITS_PER_STAGE = 8


def _attn_kernel(q_ref, kc_ref, kp_ref, vc_ref, vp_ref, o_ref, lse_ref, o_scr, lse_scr):
    has_prev = pl.program_id(1) > 0
    dil, lq = q_ref.shape[0], q_ref.shape[1]
    nsub = lq // BLOCK
    row = lax.broadcasted_iota(I32, (BLOCK, BLOCK), 0)
    col = lax.broadcasted_iota(I32, (BLOCK, BLOCK), 1)
    cur_mask = col <= row
    prev_mask = col >= row
    nt = (((1,), (1,)), ((), ()))
    rows = lambda j: slice(j * BLOCK, (j + 1) * BLOCK)
    lanes = lambda h: slice(h * HEAD_DIM, (h + 1) * HEAD_DIM)

    def attend(units):
        n = range(len(units))
        q = [q_ref[r, rows(j), lanes(h)] for r, j, h in units]
        k_cur = [kc_ref[r, rows(j), lanes(h)] for r, j, h in units]
        v_cur = [vc_ref[r, rows(j), lanes(h)] for r, j, h in units]
        k_prev = [kp_ref[r, :, lanes(h)] if j == 0 else kc_ref[r, rows(j - 1), lanes(h)] for r, j, h in units]
        v_prev = [vp_ref[r, :, lanes(h)] if j == 0 else vc_ref[r, rows(j - 1), lanes(h)] for r, j, h in units]
        s_cur = [lax.dot_general(q[i], k_cur[i], nt, preferred_element_type=F32) for i in n]
        s_prev = [lax.dot_general(q[i], k_prev[i], nt, preferred_element_type=F32) for i in n]
        s_cur = [jnp.where(cur_mask, s, NEG_INF) for s in s_cur]
        s_prev = [jnp.where(prev_mask, s, NEG_INF) for s in s_prev]
        s_prev = [jnp.where(has_prev, s_prev[i], NEG_INF) if units[i][1] == 0 else s_prev[i] for i in n]
        m = [jnp.maximum(jnp.max(s_cur[i], axis=1, keepdims=True), jnp.max(s_prev[i], axis=1, keepdims=True))
             for i in n]
        p_cur = [jnp.exp(s_cur[i] - m[i]) for i in n]
        p_prev = [jnp.exp(s_prev[i] - m[i]) for i in n]
        l = [jnp.sum(p_cur[i], axis=1, keepdims=True) + jnp.sum(p_prev[i], axis=1, keepdims=True) for i in n]
        o = [jnp.dot(p_cur[i].astype(BF16), v_cur[i], preferred_element_type=F32)
             + jnp.dot(p_prev[i].astype(BF16), v_prev[i], preferred_element_type=F32) for i in n]
        lse = [m[i] + jnp.log(l[i]) for i in n]
        for i0 in range(0, len(units), HEADS_PER_GROUP):
            r, j, _ = units[i0]
            tok = pl.ds(j * BLOCK * dil + r, BLOCK, stride=dil)
            lse_tile = jnp.zeros((BLOCK, LANES), F32)
            for i in range(i0, i0 + HEADS_PER_GROUP):
                h = units[i][2]
                o_scr[h, tok, :] = o[i] / l[i]
                lse_tile = jnp.where(col == h, lse[i], lse_tile)
            lse_scr[0, tok, :] = lse_tile

    per_residue = [(j, h) for j in range(nsub) for h in range(HEADS_PER_GROUP)]
    n_res = max(1, ATTN_UNITS_PER_STAGE // len(per_residue))
    assert dil % n_res == 0

    def residues(it, carry):
        for u0 in range(0, n_res * len(per_residue), ATTN_UNITS_PER_STAGE):
            flat = [(it * n_res + rr, j, h) for rr in range(n_res) for j, h in per_residue]
            attend(flat[u0:u0 + ATTN_UNITS_PER_STAGE])
        return carry

    if dil == n_res:
        residues(0, 0)
    else:
        lax.fori_loop(0, dil // n_res, residues, 0)
    for h in range(HEADS_PER_GROUP):
        o_ref[:, h * HEAD_DIM:(h + 1) * HEAD_DIM] = o_scr[h].astype(o_ref.dtype)
    lse_ref[...] = lse_scr[0]


def dilated_attention(qkv, *, group, batch, seq, lq):
    window, dil = ATTN_GROUPS[group]
    assert window // dil == BLOCK
    n_strided = seq // dil
    assert seq % dil == 0 and n_strided % BLOCK == 0
    lq = min(n_strided, lq)
    sub = lq // BLOCK
    tile = lq * dil
    qv = qkv.reshape(batch, dil, n_strided, qkv.shape[1])

    def cur(colblk):
        return pl.BlockSpec((None, dil, lq, GROUP_WIDTH), lambda b, i: (b, 0, i, colblk))

    def prev(colblk):
        return pl.BlockSpec((None, dil, BLOCK, GROUP_WIDTH), lambda b, i: (b, 0, jnp.maximum(i * sub - 1, 0), colblk))

    t = batch * seq
    per_batch = seq // tile
    return pl.pallas_call(
        _attn_kernel,
        grid=(batch, per_batch),
        in_specs=[cur(0), cur(1), prev(1), cur(2), prev(2)],
        out_specs=[pl.BlockSpec((tile, GROUP_WIDTH), lambda b, i: (b * per_batch + i, 0)),
                   pl.BlockSpec((tile, LANES), lambda b, i: (b * per_batch + i, 0))],
        out_shape=[jax.ShapeDtypeStruct((t, GROUP_WIDTH), BF16), jax.ShapeDtypeStruct((t, LANES), F32)],
        scratch_shapes=[pltpu.VMEM((HEADS_PER_GROUP, tile, HEAD_DIM), F32), pltpu.VMEM((1, tile, LANES), F32)],
        compiler_params=_params("arbitrary", "arbitrary"),
        name=f"dilated_attention_g{group}",
    )(qv, qv, qv, qv, qv)


CONV_ROW_CHUNK = 128
CONV_LANE_CHUNK = 128


def _conv_kernel(vc_ref, gc_ref, vp_ref, gp_ref, w_ref, cb_ref, lg_ref, lb_ref, o_ref, u_scr, y_scr):
    bs, ch = vc_ref.shape
    halo = vp_ref.shape[0]
    u_prev = vp_ref[...].astype(F32) * jax.nn.sigmoid(gp_ref[...].astype(F32))
    u_scr[0:halo, :] = jnp.where(pl.program_id(1) > 0, u_prev, 0.0)
    u_scr[halo:, :] = vc_ref[...].astype(F32) * jax.nn.sigmoid(gc_ref[...].astype(F32))
    rc, lc = CONV_ROW_CHUNK, CONV_LANE_CHUNK
    for r0 in range(0, bs, rc):
        for c0 in range(0, ch, lc):
            cs = slice(c0, c0 + lc)
            acc = None
            for b in range(SUBLANES):
                part = None
                for a in range(-(-CONV_WIDTH // SUBLANES)):
                    k = SUBLANES * a + b
                    if k >= CONV_WIDTH:
                        continue
                    j = CONV_WIDTH - 1 - k
                    lo = halo + r0 - SUBLANES * (a + 1)
                    term = w_ref[j:j + 1, cs] * u_scr[lo:lo + rc + SUBLANES, cs]
                    part = term if part is None else part + term
                shifted = part[SUBLANES - b:SUBLANES - b + rc]
                acc = shifted if acc is None else acc + shifted
            y_scr[r0:r0 + rc, cs] = acc + cb_ref[:, cs]
    y = y_scr[...]
    mu = jnp.mean(y, axis=-1, keepdims=True)
    yc = y - mu
    var = jnp.mean(yc * yc, axis=-1, keepdims=True)
    z = yc * lax.rsqrt(var + EPS) * lg_ref[...] + lb_ref[...]
    o_ref[...] = (z * jax.nn.sigmoid(z)).astype(o_ref.dtype)


def conv_branch(rest, conv_w, conv_b, ln_g, ln_b, *, batch, seq, ch, bs):
    bs = min(bs, seq)
    assert CONV_HALO >= SUBLANES * (-(-CONV_WIDTH // SUBLANES)) and bs % CONV_ROW_CHUNK == 0
    rv = rest.reshape(batch, seq, rest.shape[1])
    nh = bs // CONV_HALO

    def cur(colblk):
        return pl.BlockSpec((None, bs, ch), lambda b, i: (b, i, colblk))

    def prev(colblk):
        return pl.BlockSpec((None, CONV_HALO, ch), lambda b, i: (b, jnp.maximum(i * nh - 1, 0), colblk))

    vec = pl.BlockSpec((1, ch), lambda b, i: (0, 0))
    out = pl.pallas_call(
        _conv_kernel,
        grid=(batch, seq // bs),
        in_specs=[cur(0), cur(1), prev(0), prev(1),
                  pl.BlockSpec((CONV_WIDTH, ch), lambda b, i: (0, 0)), vec, vec, vec],
        out_specs=pl.BlockSpec((None, bs, ch), lambda b, i: (b, i, 0)),
        out_shape=jax.ShapeDtypeStruct((batch, seq, ch), BF16),
        scratch_shapes=[pltpu.VMEM((CONV_HALO + bs, ch), F32), pltpu.VMEM((bs, ch), F32)],
        compiler_params=_params("arbitrary", "arbitrary"),
        name="conv_branch",
    )(rv, rv, rv, rv, conv_w, conv_b, ln_g, ln_b)
    return out.reshape(batch * seq, ch)


def _merge_kernel(o0_ref, o1_ref, o2_ref, l0_ref, l1_ref, l2_ref, u_ref, ga_ref, gc_ref,
                  wa_ref, wc_ref, ba_ref, bc_ref, out_ref, os_scr):
    @pl.when(pl.program_id(1) == 0)
    def _():
        l0, l1, l2 = l0_ref[...], l1_ref[...], l2_ref[...]
        m = jnp.maximum(jnp.maximum(l0, l1), l2)
        es = [jnp.exp(l0 - m), jnp.exp(l1 - m), jnp.exp(l2 - m)]
        inv = 1.0 / (es[0] + es[1] + es[2])
        for g, o_ref in enumerate((o0_ref, o1_ref, o2_ref)):
            alpha = es[g] * inv
            for h in range(HEADS_PER_GROUP):
                src = slice(h * HEAD_DIM, (h + 1) * HEAD_DIM)
                dst = slice((g * HEADS_PER_GROUP + h) * HEAD_DIM, (g * HEADS_PER_GROUP + h + 1) * HEAD_DIM)
                os_scr[:, dst] = (o_ref[:, src].astype(F32) * alpha[:, h:h + 1]).astype(os_scr.dtype)

    ya = jnp.dot(os_scr[...], wa_ref[...], preferred_element_type=F32)
    yc = jnp.dot(u_ref[...], wc_ref[...], preferred_element_type=F32)
    gate_a = jax.nn.sigmoid(ga_ref[...].astype(F32) + ba_ref[...])
    gate_c = jax.nn.sigmoid(gc_ref[...].astype(F32) + bc_ref[...])
    out_ref[...] = (gate_a * ya + gate_c * yc).astype(out_ref.dtype)


def gated_merge(outs, lses, u, rest, wa, wc, ba, bc, *, d, bm, bn):
    t = u.shape[0]
    ch = u.shape[1]
    bm, bn = min(bm, t), min(bn, d)
    ga_blk = (2 * ch) // bn
    gc_blk = (2 * ch + d) // bn
    row = lambda w: pl.BlockSpec((bm, w), lambda i, j: (i, 0))
    return pl.pallas_call(
        _merge_kernel,
        grid=(t // bm, d // bn),
        in_specs=[row(GROUP_WIDTH)] * 3 + [row(LANES)] * 3 + [
            row(ch),
            pl.BlockSpec((bm, bn), lambda i, j: (i, ga_blk + j)),
            pl.BlockSpec((bm, bn), lambda i, j: (i, gc_blk + j)),
            pl.BlockSpec((ATTN_WIDTH, bn), lambda i, j: (0, j)),
            pl.BlockSpec((ch, bn), lambda i, j: (0, j)),
            pl.BlockSpec((1, bn), lambda i, j: (0, j)),
            pl.BlockSpec((1, bn), lambda i, j: (0, j)),
        ],
        out_specs=pl.BlockSpec((bm, bn), lambda i, j: (i, j)),
        out_shape=jax.ShapeDtypeStruct((t, d), BF16),
        scratch_shapes=[pltpu.VMEM((bm, ATTN_WIDTH), BF16)],
        compiler_params=_params("arbitrary", "arbitrary"),
        name="gated_merge",
    )(*outs, *lses, u, rest, rest, wa, wc, ba, bc)


def _proj_residual_kernel(x_ref, a_ref, w_ref, o_ref):
    o_ref[...] = x_ref[...] + jnp.dot(a_ref[...], w_ref[...], preferred_element_type=F32)


def proj_residual(x, a, w, *, bm, bn):
    t, d = x.shape
    k = a.shape[1]
    bm, bn = min(bm, t), min(bn, d)
    return pl.pallas_call(
        _proj_residual_kernel,
        grid=(t // bm, d // bn),
        in_specs=[
            pl.BlockSpec((bm, bn), lambda i, j: (i, j)),
            pl.BlockSpec((bm, k), lambda i, j: (i, 0)),
            pl.BlockSpec((k, bn), lambda i, j: (0, j)),
        ],
        out_specs=pl.BlockSpec((bm, bn), lambda i, j: (i, j)),
        out_shape=jax.ShapeDtypeStruct((t, d), F32),
        compiler_params=_params("arbitrary", "arbitrary"),
        name="proj_residual",
    )(x, a, w)


def _swiglu_step(h, w1_ref, w3_ref, w2_ref):
    a = jnp.dot(h, w1_ref[...], preferred_element_type=F32)
    b = jnp.dot(h, w3_ref[...], preferred_element_type=F32)
    act = (a * jax.nn.sigmoid(a) * b).astype(BF16)
    return jnp.dot(act, w2_ref[...], preferred_element_type=F32)


def _ffn_kernel(x_ref, g_ref, w1_ref, w3_ref, w2_ref, o_ref, h_scr):
    @pl.when(pl.program_id(1) == 0)
    def _():
        _norm_rows_to(h_scr, x_ref, g_ref)
        o_ref[...] = x_ref[...]

    o_ref[...] += _swiglu_step(h_scr[...], w1_ref, w3_ref, w2_ref)


def dense_ffn(x, g, w1, w3, w2, *, bm, bf):
    t, d = x.shape
    f = w1.shape[1]
    bm, bf = min(bm, t), min(bf, f)
    return pl.pallas_call(
        _ffn_kernel,
        grid=(t // bm, f // bf),
        in_specs=[
            pl.BlockSpec((bm, d), lambda i, k: (i, 0)),
            pl.BlockSpec((1, d), lambda i, k: (0, 0)),
            pl.BlockSpec((d, bf), lambda i, k: (0, k)),
            pl.BlockSpec((d, bf), lambda i, k: (0, k)),
            pl.BlockSpec((bf, d), lambda i, k: (k, 0)),
        ],
        out_specs=pl.BlockSpec((bm, d), lambda i, k: (i, 0)),
        out_shape=jax.ShapeDtypeStruct((t, d), F32),
        scratch_shapes=[pltpu.VMEM((bm, d), BF16)],
        compiler_params=_params("arbitrary", "arbitrary"),
        name="dense_ffn",
    )(x, g, w1, w3, w2)


def _router_kernel(x_ref, g_ref, r_ref, tri_ref, hn_ref, dest_ref, gate_ref, cnt_ref, carry_scr, *, capacity):
    @pl.when(pl.program_id(0) == 0)
    def _():
        carry_scr[...] = jnp.zeros_like(carry_scr)

    h = _rms_rows(x_ref[...], g_ref[...])
    hn_ref[...] = h
    logits = jnp.dot(h, r_ref[...], preferred_element_type=F32, precision=lax.Precision.HIGHEST)
    lane = lax.broadcasted_iota(I32, logits.shape, 1)
    logits = jnp.where(lane < N_EXPERTS, logits, NEG_INF)
    m1 = jnp.max(logits, axis=1, keepdims=True)
    i1 = jnp.min(jnp.where(logits == m1, lane, LANES), axis=1, keepdims=True)
    hot1 = lane == i1
    rest = jnp.where(hot1, NEG_INF, logits)
    m2 = jnp.max(rest, axis=1, keepdims=True)
    i2 = jnp.min(jnp.where(rest == m2, lane, LANES), axis=1, keepdims=True)
    hot2 = lane == i2
    e = jnp.exp(m2 - m1)
    w1 = 1.0 / (1.0 + e)
    w2 = e * w1
    sel = jnp.where(hot1, 1.0, 0.0) + jnp.where(hot2, 1.0, 0.0)
    rank = carry_scr[...] + jnp.dot(tri_ref[...], sel.astype(BF16), preferred_element_type=F32)
    r1 = jnp.sum(jnp.where(hot1, rank, 0.0), axis=1, keepdims=True).astype(I32)
    r2 = jnp.sum(jnp.where(hot2, rank, 0.0), axis=1, keepdims=True).astype(I32)
    carry_scr[...] = carry_scr[...] + jnp.sum(sel, axis=0, keepdims=True)
    dest_ref[...] = jnp.where(lane == 0, i1 * capacity + r1, jnp.where(lane == 1, i2 * capacity + r2, 0))
    gate_ref[...] = jnp.where(lane == 0, w1, jnp.where(lane == 1, w2, 0.0))
    cnt_ref[...] = jnp.broadcast_to(carry_scr[...], cnt_ref.shape)


def moe_router(x, g, router, *, bm):
    t, d = x.shape
    bm = min(bm, t)
    r_pad = jnp.zeros((d, LANES), F32).at[:, :N_EXPERTS].set(router)
    tri = (lax.broadcasted_iota(I32, (bm, bm), 0) > lax.broadcasted_iota(I32, (bm, bm), 1)).astype(BF16)
    return pl.pallas_call(
        functools.partial(_router_kernel, capacity=t),
        grid=(t // bm,),
        in_specs=[
            pl.BlockSpec((bm, d), lambda i: (i, 0)),
            pl.BlockSpec((1, d), lambda i: (0, 0)),
            pl.BlockSpec((d, LANES), lambda i: (0, 0)),
            pl.BlockSpec((bm, bm), lambda i: (0, 0)),
        ],
        out_specs=[
            pl.BlockSpec((bm, d), lambda i: (i, 0)),
            pl.BlockSpec((bm, LANES), lambda i: (i, 0)),
            pl.BlockSpec((bm, LANES), lambda i: (i, 0)),
            pl.BlockSpec((8, LANES), lambda i: (0, 0)),
        ],
        out_shape=[
            jax.ShapeDtypeStruct((t, d), F32),
            jax.ShapeDtypeStruct((t, LANES), I32),
            jax.ShapeDtypeStruct((t, LANES), F32),
            jax.ShapeDtypeStruct((8, LANES), F32),
        ],
        scratch_shapes=[pltpu.VMEM((1, LANES), F32)],
        compiler_params=_params("arbitrary"),
        name="moe_router",
    )(x, g, r_pad, tri)


def _row_copy(src_ref, src_row, dst_ref, dst_row, sem):
    return pltpu.make_async_copy(src_ref.at[pl.ds(src_row, 1)], dst_ref.at[pl.ds(dst_row, 1)], sem)


def _dispatch_kernel(cnt_ref, d0_ref, d1_ref, hn_ref, xs_ref, zero_scr, sem, pad_sem, *, capacity, row_tile):
    bm = hn_ref.shape[0]

    def pad_copies(do):
        for e in range(N_EXPERTS):
            cnt = cnt_ref[e]
            pad = (-cnt) & (row_tile - 1)
            head = pad & (SUBLANES - 1)
            off = e * capacity + cnt
            for b in range(SUBLANES - 1):
                @pl.when(b < head)
                def _(off=off, b=b):
                    do(pltpu.make_async_copy(zero_scr.at[pl.ds(0, 1)], xs_ref.at[pl.ds(off + b, 1)], pad_sem))
            off = off + head
            size = SUBLANES
            while size < row_tile:
                @pl.when((pad & size) != 0)
                def _(off=off, size=size):
                    dst = xs_ref.at[pl.ds(pl.multiple_of(off, SUBLANES), size)]
                    do(pltpu.make_async_copy(zero_scr.at[pl.ds(0, size)], dst, pad_sem))
                off = off + (pad & size)
                size *= 2

    @pl.when(pl.program_id(0) == 0)
    def _():
        zero_scr[...] = jnp.zeros_like(zero_scr)
        pad_copies(lambda cp: cp.start())
        pad_copies(lambda cp: cp.wait())

    def issue(t, carry):
        _row_copy(hn_ref, t, xs_ref, d0_ref[0, 0, t], sem).start()
        _row_copy(hn_ref, t, xs_ref, d1_ref[0, 0, t], sem).start()
        return carry

    def drain(t, carry):
        _row_copy(hn_ref, 0, xs_ref, 0, sem).wait()
        _row_copy(hn_ref, 0, xs_ref, 0, sem).wait()
        return carry

    lax.fori_loop(0, bm, issue, 0)
    lax.fori_loop(0, bm, drain, 0)


def moe_dispatch(counts, dest0, dest1, hn, *, bm, row_tile):
    t, d = hn.shape
    bm = min(bm, t)
    nt = t // bm
    idx_spec = pl.BlockSpec((1, 1, bm), lambda i, cnt: (i, 0, 0), memory_space=pltpu.SMEM)
    return pl.pallas_call(
        functools.partial(_dispatch_kernel, capacity=t, row_tile=row_tile),
        grid_spec=pltpu.PrefetchScalarGridSpec(
            num_scalar_prefetch=1,
            grid=(nt,),
            in_specs=[idx_spec, idx_spec, pl.BlockSpec((bm, d), lambda i, cnt: (i, 0))],
            out_specs=pl.BlockSpec(memory_space=pl.ANY),
            scratch_shapes=[pltpu.VMEM((row_tile // 2, d), F32),
                            pltpu.SemaphoreType.DMA(()), pltpu.SemaphoreType.DMA(())],
        ),
        out_shape=jax.ShapeDtypeStruct((N_EXPERTS * t, d), F32),
        compiler_params=_params("arbitrary"),
        name="moe_dispatch",
    )(counts, dest0.reshape(nt, 1, bm), dest1.reshape(nt, 1, bm), hn)


def _moe_ffn_kernel(te_ref, tb_ref, tv_ref, x_ref, w1_ref, w3_ref, w2_ref, o_ref, h_scr):
    @pl.when(tv_ref[pl.program_id(0)] != 0)
    def _():
        @pl.when(pl.program_id(1) == 0)
        def _():
            h_scr[...] = x_ref[...].astype(h_scr.dtype)
            o_ref[...] = jnp.zeros_like(o_ref)

        o_ref[...] += _swiglu_step(h_scr[...], w1_ref, w3_ref, w2_ref)


def moe_grouped_ffn(tile_expert, tile_block, tile_valid, xs, w1, w3, w2, *, row_tile, bf):
    rows, d = xs.shape
    f = w1.shape[2]
    bf = min(bf, f)
    nf = f // bf
    n_steps = tile_expert.shape[0]

    def fk(s, k, tv):
        return jnp.where(tv[s] != 0, k, nf - 1)

    return pl.pallas_call(
        _moe_ffn_kernel,
        grid_spec=pltpu.PrefetchScalarGridSpec(
            num_scalar_prefetch=3,
            grid=(n_steps, nf),
            in_specs=[
                pl.BlockSpec((row_tile, d), lambda s, k, te, tb, tv: (tb[s], 0)),
                pl.BlockSpec((None, d, bf), lambda s, k, te, tb, tv: (te[s], 0, fk(s, k, tv))),
                pl.BlockSpec((None, d, bf), lambda s, k, te, tb, tv: (te[s], 0, fk(s, k, tv))),
                pl.BlockSpec((None, bf, d), lambda s, k, te, tb, tv: (te[s], fk(s, k, tv), 0)),
            ],
            out_specs=pl.BlockSpec((row_tile, d), lambda s, k, te, tb, tv: (tb[s], 0)),
            scratch_shapes=[pltpu.VMEM((row_tile, d), BF16)],
        ),
        out_shape=jax.ShapeDtypeStruct((rows, d), F32),
        compiler_params=_params("arbitrary", "arbitrary"),
        name="moe_grouped_ffn",
    )(tile_expert, tile_block, tile_valid, xs, w1, w3, w2)


def _combine_kernel(d0_ref, d1_ref, x_ref, gate_ref, ys_ref, o_ref, buf, sem):
    bm = x_ref.shape[0]

    def issue(t, carry):
        _row_copy(ys_ref, d0_ref[0, 0, t], buf.at[0], t, sem).start()
        _row_copy(ys_ref, d1_ref[0, 0, t], buf.at[1], t, sem).start()
        return carry

    def drain(t, carry):
        _row_copy(ys_ref, 0, buf.at[0], 0, sem).wait()
        _row_copy(ys_ref, 0, buf.at[1], 0, sem).wait()
        return carry

    lax.fori_loop(0, bm, issue, 0)
    lax.fori_loop(0, bm, drain, 0)
    gates = gate_ref[...]
    o_ref[...] = x_ref[...] + gates[:, 0:1] * buf[0] + gates[:, 1:2] * buf[1]


def moe_combine(dest0, dest1, x, gates, ys, *, bm):
    t, d = x.shape
    bm = min(bm, t)
    nt = t // bm
    idx_spec = pl.BlockSpec((1, 1, bm), lambda i: (i, 0, 0), memory_space=pltpu.SMEM)
    return pl.pallas_call(
        _combine_kernel,
        grid=(nt,),
        in_specs=[idx_spec, idx_spec,
                  pl.BlockSpec((bm, d), lambda i: (i, 0)),
                  pl.BlockSpec((bm, LANES), lambda i: (i, 0)),
                  pl.BlockSpec(memory_space=pl.ANY)],
        out_specs=pl.BlockSpec((bm, d), lambda i: (i, 0)),
        out_shape=jax.ShapeDtypeStruct((t, d), F32),
        scratch_shapes=[pltpu.VMEM((2, bm, d), F32), pltpu.SemaphoreType.DMA(())],
        compiler_params=_params("arbitrary"),
        name="moe_combine",
    )(dest0.reshape(nt, 1, bm), dest1.reshape(nt, 1, bm), x, gates, ys)


def _tile_tables(counts, *, capacity, row_tile, n_steps):
    tiles = (counts + row_tile - 1) // row_tile
    ends = jnp.cumsum(tiles)
    total = ends[-1]
    step = jnp.minimum(jnp.arange(n_steps, dtype=I32), total - 1)
    expert = jnp.sum((step[:, None] >= ends[None, :]).astype(I32), axis=1)
    block = expert * (capacity // row_tile) + step - (ends - tiles)[expert]
    valid = (jnp.arange(n_steps, dtype=I32) < total).astype(I32)
    return expert.astype(I32), block.astype(I32), valid


def moe_ffn(x, g, router, w1, w3, w2, *, bm, row_tile, bf):
    t, d = x.shape
    row_tile = min(row_tile, t)
    hn, dest, gates, cnt = moe_router(x, g, router, bm=bm)
    counts = cnt[0, :N_EXPERTS].astype(I32)
    dest0, dest1 = dest[:, 0], dest[:, 1]
    n_steps = (2 * t) // row_tile + N_EXPERTS
    tile_expert, tile_block, tile_valid = _tile_tables(counts, capacity=t, row_tile=row_tile, n_steps=n_steps)
    xs = moe_dispatch(counts, dest0, dest1, hn, bm=bm, row_tile=row_tile)
    ys = moe_grouped_ffn(tile_expert, tile_block, tile_valid, xs, w1, w3, w2, row_tile=row_tile, bf=bf)
    return moe_combine(dest0, dest1, x, gates, ys, bm=bm)


def kernel(x, norm_mix, w_in, q_norm, k_norm, w_attn_o, conv_w, conv_b, conv_ln_g, conv_ln_b, w_conv_o, gate_b,
           w_out, norm_ffn, ffn_w1, ffn_w3, ffn_w2, router, moe_w1, moe_w3, moe_w2):
    batch, seq, d = x.shape
    depth = w_in.shape[0]
    ch = conv_w.shape[-1]
    t = batch * seq
    dilations = tuple(dil for _, dil in ATTN_GROUPS)
    strided = tuple(dil for dil in dilations if dil > 1)
    tabs = [rope_table(seq, dil) for dil in dilations]
    row = lambda v: v.reshape(1, -1).astype(F32)
    xt = x.reshape(t, d)
    for l in range(depth):
        w_in_l = w_in[l]
        hs = norm_permute(xt, row(norm_mix[l]), batch=batch, seq=seq, dilations=strided, bm=512)
        h_by_dil = dict(zip((1,) + strided, hs))
        outs, lses = [], []
        for grp, dil in enumerate(dilations):
            cols = [slice(part * ATTN_WIDTH + grp * GROUP_WIDTH, part * ATTN_WIDTH + (grp + 1) * GROUP_WIDTH)
                    for part in range(3)]
            w_g = jnp.concatenate([w_in_l[:, c] for c in cols], axis=1).astype(BF16)
            qkv_g = qkv_proj(h_by_dil[dil], w_g, tabs[grp], row(q_norm[l]), row(k_norm[l]), seq=seq, bm=1024)
            o_g, lse_g = dilated_attention(qkv_g, group=grp, batch=batch, seq=seq, lq=2048 // max(dil, 4))
            outs.append(o_g)
            lses.append(lse_g)
        rest = matmul(h_by_dil[1], w_in_l[:, 3 * ATTN_WIDTH:].astype(BF16), bm=1024, bn=2048)
        u = conv_branch(rest, conv_w[l].reshape(CONV_WIDTH, ch), row(conv_b[l]), row(conv_ln_g[l]),
                        row(conv_ln_b[l]), batch=batch, seq=seq, ch=ch, bs=256)
        merged = gated_merge(outs, lses, u, rest, w_attn_o[l].astype(BF16), w_conv_o[l].astype(BF16),
                             row(gate_b[l, :d]), row(gate_b[l, d:]), d=d, bm=1024, bn=512)
        xt = proj_residual(xt, merged, w_out[l].astype(BF16), bm=512, bn=2048)
        g_ffn = row(norm_ffn[l])
        i = l // 2
        if l % 2 == 0:
            xt = dense_ffn(xt, g_ffn, ffn_w1[i].astype(BF16), ffn_w3[i].astype(BF16), ffn_w2[i].astype(BF16),
                           bm=512, bf=1024)
        else:
            xt = moe_ffn(xt, g_ffn, router[i], moe_w1[i].astype(BF16), moe_w3[i].astype(BF16),
                         moe_w2[i].astype(BF16), bm=512, row_tile=512, bf=1024)
    return xt.reshape(batch, seq, d)
```

```python
import functools

import jax
import jax.numpy as jnp
from jax import lax
from jax.experimental import pallas as pl
from jax.experimental.pallas import tpu as pltpu

F32 = jnp.float32
BF16 = jnp.bfloat16
I32 = jnp.int32
U32 = jnp.uint32

HEAD_DIM = 128
ATTN_GROUPS = ((128, 1), (512, 4), (2048, 16))
HEADS_PER_GROUP = 4
GROUP_WIDTH = HEADS_PER_GROUP * HEAD_DIM
ATTN_WIDTH = len(ATTN_GROUPS) * GROUP_WIDTH
ROPE_DIM = HEAD_DIM // 4
ROPE_THETA = 500000.0
CONV_WIDTH = 31
N_EXPERTS = 8
BLOCK = 128
EPS = 1e-6
NEG_INF = -1e30

LANES = 128
SUBLANES = 8
CONV_HALO = 32
VMEM_LIMIT_BYTES = 56 * 1024 * 1024


def _params(*semantics):
    return pltpu.CompilerParams(dimension_semantics=semantics, vmem_limit_bytes=VMEM_LIMIT_BYTES)


def _rms_rows(x, g):
    return x * lax.rsqrt(jnp.mean(x * x, axis=-1, keepdims=True) + EPS) * g


def _norm_rows_to(h_ref, x_ref, g_ref, row_chunk=256):
    n = x_ref.shape[0]
    row_chunk = min(row_chunk, n)

    def body(c, carry):
        rows = pl.ds(pl.multiple_of(c * row_chunk, row_chunk), row_chunk)
        h_ref[rows, :] = _rms_rows(x_ref[rows, :], g_ref[...]).astype(h_ref.dtype)
        return carry

    lax.fori_loop(0, n // row_chunk, body, 0)


def _norm_permute_kernel(x_ref, g_ref, h_ref, *rest, dilations, row_chunk=256):
    perm_refs, scr = rest[:-1], rest[-1]
    bm, dm = x_ref.shape
    nl = dm // LANES
    row_chunk = min(row_chunk, bm)

    def body(c, carry):
        rows = pl.ds(pl.multiple_of(c * row_chunk, row_chunk), row_chunk)
        h = _rms_rows(x_ref[rows, :], g_ref[...])
        h_ref[rows, :] = h.astype(h_ref.dtype)
        for l in range(nl):
            scr[l, rows, :] = h[:, l * LANES:(l + 1) * LANES]
        return carry

    lax.fori_loop(0, bm // row_chunk, body, 0)
    for p_ref, dil in zip(perm_refs, dilations):
        n = bm // dil
        for r in range(dil):
            for l in range(nl):
                p_ref[r, :, l * LANES:(l + 1) * LANES] = scr[l, pl.ds(r, n, stride=dil), :].astype(p_ref.dtype)


def norm_permute(x, g, *, batch, seq, dilations, bm):
    t, d = x.shape
    bm = min(bm, seq)
    xv = x.reshape(batch, seq, d)
    outs = pl.pallas_call(
        functools.partial(_norm_permute_kernel, dilations=dilations),
        grid=(batch, seq // bm),
        in_specs=[pl.BlockSpec((None, bm, d), lambda b, i: (b, i, 0)),
                  pl.BlockSpec((1, d), lambda b, i: (0, 0))],
        out_specs=[pl.BlockSpec((None, bm, d), lambda b, i: (b, i, 0))] + [
            pl.BlockSpec((None, dil, bm // dil, d), lambda b, i: (b, 0, i, 0)) for dil in dilations],
        out_shape=[jax.ShapeDtypeStruct((batch, seq, d), BF16)] + [
            jax.ShapeDtypeStruct((batch, dil, seq // dil, d), BF16) for dil in dilations],
        scratch_shapes=[pltpu.VMEM((d // LANES, bm, LANES), F32)],
        compiler_params=_params("arbitrary", "arbitrary"),
        name="norm_permute",
    )(xv, g)
    return [o.reshape(t, d) for o in outs]


def _matmul_kernel(a_ref, w_ref, o_ref):
    o_ref[...] = jnp.dot(a_ref[...], w_ref[...], preferred_element_type=F32).astype(o_ref.dtype)


def matmul(a, w, *, bm, bn):
    t, k = a.shape
    n = w.shape[1]
    bm, bn = min(bm, t), min(bn, n)
    return pl.pallas_call(
        _matmul_kernel,
        grid=(t // bm, n // bn),
        in_specs=[pl.BlockSpec((bm, k), lambda i, j: (i, 0)),
                  pl.BlockSpec((k, bn), lambda i, j: (0, j))],
        out_specs=pl.BlockSpec((bm, bn), lambda i, j: (i, j)),
        out_shape=jax.ShapeDtypeStruct((t, n), BF16),
        compiler_params=_params("arbitrary", "arbitrary"),
        name="matmul",
    )(a, w)


def _norm_rope(t, g, tab):
    y = _rms_rows(t, g)
    half = ROPE_DIM // 2
    return (y * tab[:, :LANES]
            + pltpu.roll(y, LANES - half, 1) * tab[:, LANES:2 * LANES]
            + pltpu.roll(y, half, 1) * tab[:, 2 * LANES:])


def _qkv_epilogue(acc_ref, tab_ref, qg_ref, kg_ref, o_ref):
    tab = tab_ref[...]
    for h in range(HEADS_PER_GROUP):
        qs = slice(h * HEAD_DIM, (h + 1) * HEAD_DIM)
        ks = slice(GROUP_WIDTH + h * HEAD_DIM, GROUP_WIDTH + (h + 1) * HEAD_DIM)
        o_ref[:, qs] = (_norm_rope(acc_ref[:, qs], qg_ref[...], tab) * (HEAD_DIM ** -0.5)).astype(o_ref.dtype)
        o_ref[:, ks] = _norm_rope(acc_ref[:, ks], kg_ref[...], tab).astype(o_ref.dtype)
    o_ref[:, 2 * GROUP_WIDTH:] = acc_ref[:, 2 * GROUP_WIDTH:].astype(o_ref.dtype)


def _qkv_proj_kernel(h_ref, w_ref, tab_ref, qg_ref, kg_ref, o_ref, acc_even, acc_odd):
    i = pl.program_id(0)

    @pl.when(i == 0)
    def _():
        acc_odd[...] = jnp.zeros_like(acc_odd)

    def step(acc_new, acc_done):
        acc_new[...] = jnp.dot(h_ref[...], w_ref[...], preferred_element_type=F32)
        _qkv_epilogue(acc_done, tab_ref, qg_ref, kg_ref, o_ref)

    @pl.when(i % 2 == 0)
    def _():
        step(acc_even, acc_odd)

    @pl.when(i % 2 == 1)
    def _():
        step(acc_odd, acc_even)


def qkv_proj(h, w, tab, qg, kg, *, seq, bm):
    t, d = h.shape
    n = w.shape[1]
    bm = min(bm, seq)
    per_batch = seq // bm
    n_tiles = t // bm
    done = lambda i: jnp.maximum(i - 1, 0)
    return pl.pallas_call(
        _qkv_proj_kernel,
        grid=(n_tiles + 1,),
        in_specs=[pl.BlockSpec((bm, d), lambda i: (jnp.minimum(i, n_tiles - 1), 0)),
                  pl.BlockSpec((d, n), lambda i: (0, 0)),
                  pl.BlockSpec((bm, tab.shape[1]), lambda i: (done(i) % per_batch, 0)),
                  pl.BlockSpec((1, HEAD_DIM), lambda i: (0, 0)),
                  pl.BlockSpec((1, HEAD_DIM), lambda i: (0, 0))],
        out_specs=pl.BlockSpec((bm, n), lambda i: (done(i), 0)),
        out_shape=jax.ShapeDtypeStruct((t, n), BF16),
        scratch_shapes=[pltpu.VMEM((bm, n), F32), pltpu.VMEM((bm, n), F32)],
        compiler_params=_params("arbitrary"),
        name="qkv_proj",
    )(h, w, tab, qg, kg)


def rope_table(seq, dil):
    half = ROPE_DIM // 2
    inv = jnp.power(jnp.float32(ROPE_THETA), -jnp.arange(half, dtype=F32) * 2.0 / ROPE_DIM)
    pos = (jnp.arange(seq // dil, dtype=I32)[None, :] * dil + jnp.arange(dil, dtype=I32)[:, None]).reshape(seq)
    ang = pos.astype(F32)[:, None] * inv[None, :]
    cos, sin = jnp.cos(ang), jnp.sin(ang)
    ones = jnp.ones((seq, HEAD_DIM - ROPE_DIM), F32)
    zeros_tail = jnp.zeros((seq, HEAD_DIM - half), F32)
    c = jnp.concatenate([cos, cos, ones], axis=1)
    s_lo = jnp.concatenate([-sin, zeros_tail], axis=1)
    s_hi = jnp.concatenate([jnp.zeros((seq, half), F32), sin, jnp.zeros((seq, HEAD_DIM - ROPE_DIM), F32)], axis=1)
    return jnp.concatenate([c, s_lo, s_hi], axis=1)


ATTN_UNITS_PER_STAGE = 8


def _attn_kernel(q_ref, kc_ref, kp_ref, vc_ref, vp_ref, o_ref, lse_ref, o_scr, lse_scr):
    has_prev = pl.program_id(1) > 0
    dil, lq = q_ref.shape[0], q_ref.shape[1]
    nsub = lq // BLOCK
    row = lax.broadcasted_iota(I32, (BLOCK, BLOCK), 0)
    col = lax.broadcasted_iota(I32, (BLOCK, BLOCK), 1)
    cur_mask = col <= row
    prev_mask = col >= row
    nt = (((1,), (1,)), ((), ()))
    rows = lambda j: slice(j * BLOCK, (j + 1) * BLOCK)
    lanes = lambda h: slice(h * HEAD_DIM, (h + 1) * HEAD_DIM)

    def attend(units):
        n = range(len(units))
        q = [q_ref[r, rows(j), lanes(h)] for r, j, h in units]
        k_cur = [kc_ref[r, rows(j), lanes(h)] for r, j, h in units]
        v_cur = [vc_ref[r, rows(j), lanes(h)] for r, j, h in units]
        k_prev = [kp_ref[r, :, lanes(h)] if j == 0 else kc_ref[r, rows(j - 1), lanes(h)] for r, j, h in units]
        v_prev = [vp_ref[r, :, lanes(h)] if j == 0 else vc_ref[r, rows(j - 1), lanes(h)] for r, j, h in units]
        s_cur = [lax.dot_general(q[i], k_cur[i], nt, preferred_element_type=F32) for i in n]
        s_prev = [lax.dot_general(q[i], k_prev[i], nt, preferred_element_type=F32) for i in n]
        s_cur = [jnp.where(cur_mask, s, NEG_INF) for s in s_cur]
        s_prev = [jnp.where(prev_mask, s, NEG_INF) for s in s_prev]
        s_prev = [jnp.where(has_prev, s_prev[i], NEG_INF) if units[i][1] == 0 else s_prev[i] for i in n]
        m = [jnp.maximum(jnp.max(s_cur[i], axis=1, keepdims=True), jnp.max(s_prev[i], axis=1, keepdims=True))
             for i in n]
        p_cur = [jnp.exp(s_cur[i] - m[i]) for i in n]
        p_prev = [jnp.exp(s_prev[i] - m[i]) for i in n]
        l = [jnp.sum(p_cur[i], axis=1, keepdims=True) + jnp.sum(p_prev[i], axis=1, keepdims=True) for i in n]
        o = [jnp.dot(p_cur[i].astype(BF16), v_cur[i], preferred_element_type=F32)
             + jnp.dot(p_prev[i].astype(BF16), v_prev[i], preferred_element_type=F32) for i in n]
        lse = [m[i] + jnp.log(l[i]) for i in n]
        for i0 in range(0, len(units), HEADS_PER_GROUP):
            r, j, _ = units[i0]
            tok = pl.ds(j * BLOCK * dil + r, BLOCK, stride=dil)
            lse_tile = jnp.zeros((BLOCK, LANES), F32)
            for i in range(i0, i0 + HEADS_PER_GROUP):
                h = units[i][2]
                o_scr[h, tok, :] = o[i] / l[i]
                lse_tile = jnp.where(col == h, lse[i], lse_tile)
            lse_scr[0, tok, :] = lse_tile

    per_residue = [(j, h) for j in range(nsub) for h in range(HEADS_PER_GROUP)]
    n_res = max(1, ATTN_UNITS_PER_STAGE // len(per_residue))
    assert dil % n_res == 0

    def residues(it, carry):
        for u0 in range(0, n_res * len(per_residue), ATTN_UNITS_PER_STAGE):
            flat = [(it * n_res + rr, j, h) for rr in range(n_res) for j, h in per_residue]
            attend(flat[u0:u0 + ATTN_UNITS_PER_STAGE])
        return carry

    if dil == n_res:
        residues(0, 0)
    else:
        lax.fori_loop(0, dil // n_res, residues, 0)
    for h in range(HEADS_PER_GROUP):
        o_ref[:, h * HEAD_DIM:(h + 1) * HEAD_DIM] = o_scr[h].astype(o_ref.dtype)
    lse_ref[...] = lse_scr[0]


def dilated_attention(qkv, *, group, batch, seq, lq):
    window, dil = ATTN_GROUPS[group]
    assert window // dil == BLOCK
    n_strided = seq // dil
    assert seq % dil == 0 and n_strided % BLOCK == 0
    lq = min(n_strided, lq)
    sub = lq // BLOCK
    tile = lq * dil
    qv = qkv.reshape(batch, dil, n_strided, qkv.shape[1])

    def cur(colblk):
        return pl.BlockSpec((None, dil, lq, GROUP_WIDTH), lambda b, i: (b, 0, i, colblk))

    def prev(colblk):
        return pl.BlockSpec((None, dil, BLOCK, GROUP_WIDTH), lambda b, i: (b, 0, jnp.maximum(i * sub - 1, 0), colblk))

    t = batch * seq
    per_batch = seq // tile
    return pl.pallas_call(
        _attn_kernel,
        grid=(batch, per_batch),
        in_specs=[cur(0), cur(1), prev(1), cur(2), prev(2)],
        out_specs=[pl.BlockSpec((tile, GROUP_WIDTH), lambda b, i: (b * per_batch + i, 0)),
                   pl.BlockSpec((tile, LANES), lambda b, i: (b * per_batch + i, 0))],
        out_shape=[jax.ShapeDtypeStruct((t, GROUP_WIDTH), BF16), jax.ShapeDtypeStruct((t, LANES), F32)],
        scratch_shapes=[pltpu.VMEM((HEADS_PER_GROUP, tile, HEAD_DIM), F32), pltpu.VMEM((1, tile, LANES), F32)],
        compiler_params=_params("arbitrary", "arbitrary"),
        name=f"dilated_attention_g{group}",
    )(qv, qv, qv, qv, qv)


CONV_ROW_CHUNK = 128
CONV_LANE_CHUNK = 128


def _conv_kernel(vc_ref, gc_ref, vp_ref, gp_ref, w_ref, cb_ref, lg_ref, lb_ref, o_ref, u_scr, y_scr):
    bs, ch = vc_ref.shape
    halo = vp_ref.shape[0]
    u_prev = vp_ref[...].astype(F32) * jax.nn.sigmoid(gp_ref[...].astype(F32))
    u_scr[0:halo, :] = jnp.where(pl.program_id(1) > 0, u_prev, 0.0)
    u_scr[halo:, :] = vc_ref[...].astype(F32) * jax.nn.sigmoid(gc_ref[...].astype(F32))
    rc, lc = CONV_ROW_CHUNK, CONV_LANE_CHUNK
    for r0 in range(0, bs, rc):
        for c0 in range(0, ch, lc):
            cs = slice(c0, c0 + lc)
            acc = None
            for b in range(SUBLANES):
                part = None
                for a in range(-(-CONV_WIDTH // SUBLANES)):
                    k = SUBLANES * a + b
                    if k >= CONV_WIDTH:
                        continue
                    j = CONV_WIDTH - 1 - k
                    lo = halo + r0 - SUBLANES * (a + 1)
                    term = w_ref[j:j + 1, cs] * u_scr[lo:lo + rc + SUBLANES, cs]
                    part = term if part is None else part + term
                shifted = part[SUBLANES - b:SUBLANES - b + rc]
                acc = shifted if acc is None else acc + shifted
            y_scr[r0:r0 + rc, cs] = acc + cb_ref[:, cs]
    y = y_scr[...]
    mu = jnp.mean(y, axis=-1, keepdims=True)
    yc = y - mu
    var = jnp.mean(yc * yc, axis=-1, keepdims=True)
    z = yc * lax.rsqrt(var + EPS) * lg_ref[...] + lb_ref[...]
    o_ref[...] = (z * jax.nn.sigmoid(z)).astype(o_ref.dtype)


def conv_branch(rest, conv_w, conv_b, ln_g, ln_b, *, batch, seq, ch, bs):
    bs = min(bs, seq)
    assert CONV_HALO >= SUBLANES * (-(-CONV_WIDTH // SUBLANES)) and bs % CONV_ROW_CHUNK == 0
    rv = rest.reshape(batch, seq, rest.shape[1])
    nh = bs // CONV_HALO

    def cur(colblk):
        return pl.BlockSpec((None, bs, ch), lambda b, i: (b, i, colblk))

    def prev(colblk):
        return pl.BlockSpec((None, CONV_HALO, ch), lambda b, i: (b, jnp.maximum(i * nh - 1, 0), colblk))

    vec = pl.BlockSpec((1, ch), lambda b, i: (0, 0))
    out = pl.pallas_call(
        _conv_kernel,
        grid=(batch, seq // bs),
        in_specs=[cur(0), cur(1), prev(0), prev(1),
                  pl.BlockSpec((CONV_WIDTH, ch), lambda b, i: (0, 0)), vec, vec, vec],
        out_specs=pl.BlockSpec((None, bs, ch), lambda b, i: (b, i, 0)),
        out_shape=jax.ShapeDtypeStruct((batch, seq, ch), BF16),
        scratch_shapes=[pltpu.VMEM((CONV_HALO + bs, ch), F32), pltpu.VMEM((bs, ch), F32)],
        compiler_params=_params("arbitrary", "arbitrary"),
        name="conv_branch",
    )(rv, rv, rv, rv, conv_w, conv_b, ln_g, ln_b)
    return out.reshape(batch * seq, ch)


def _merge_kernel(o0_ref, o1_ref, o2_ref, l0_ref, l1_ref, l2_ref, u_ref, ga_ref, gc_ref,
                  wa_ref, wc_ref, ba_ref, bc_ref, out_ref, os_scr):
    @pl.when(pl.program_id(1) == 0)
    def _():
        l0, l1, l2 = l0_ref[...], l1_ref[...], l2_ref[...]
        m = jnp.maximum(jnp.maximum(l0, l1), l2)
        es = [jnp.exp(l0 - m), jnp.exp(l1 - m), jnp.exp(l2 - m)]
        inv = 1.0 / (es[0] + es[1] + es[2])
        for g, o_ref in enumerate((o0_ref, o1_ref, o2_ref)):
            alpha = es[g] * inv
            for h in range(HEADS_PER_GROUP):
                src = slice(h * HEAD_DIM, (h + 1) * HEAD_DIM)
                dst = slice((g * HEADS_PER_GROUP + h) * HEAD_DIM, (g * HEADS_PER_GROUP + h + 1) * HEAD_DIM)
                os_scr[:, dst] = (o_ref[:, src].astype(F32) * alpha[:, h:h + 1]).astype(os_scr.dtype)

    ya = jnp.dot(os_scr[...], wa_ref[...], preferred_element_type=F32)
    yc = jnp.dot(u_ref[...], wc_ref[...], preferred_element_type=F32)
    gate_a = jax.nn.sigmoid(ga_ref[...].astype(F32) + ba_ref[...])
    gate_c = jax.nn.sigmoid(gc_ref[...].astype(F32) + bc_ref[...])
    out_ref[...] = (gate_a * ya + gate_c * yc).astype(out_ref.dtype)


def gated_merge(outs, lses, u, rest, wa, wc, ba, bc, *, d, bm, bn):
    t = u.shape[0]
    ch = u.shape[1]
    bm, bn = min(bm, t), min(bn, d)
    ga_blk = (2 * ch) // bn
    gc_blk = (2 * ch + d) // bn
    row = lambda w: pl.BlockSpec((bm, w), lambda i, j: (i, 0))
    return pl.pallas_call(
        _merge_kernel,
        grid=(t // bm, d // bn),
        in_specs=[row(GROUP_WIDTH)] * 3 + [row(LANES)] * 3 + [
            row(ch),
            pl.BlockSpec((bm, bn), lambda i, j: (i, ga_blk + j)),
            pl.BlockSpec((bm, bn), lambda i, j: (i, gc_blk + j)),
            pl.BlockSpec((ATTN_WIDTH, bn), lambda i, j: (0, j)),
            pl.BlockSpec((ch, bn), lambda i, j: (0, j)),
            pl.BlockSpec((1, bn), lambda i, j: (0, j)),
            pl.BlockSpec((1, bn), lambda i, j: (0, j)),
        ],
        out_specs=pl.BlockSpec((bm, bn), lambda i, j: (i, j)),
        out_shape=jax.ShapeDtypeStruct((t, d), BF16),
        scratch_shapes=[pltpu.VMEM((bm, ATTN_WIDTH), BF16)],
        compiler_params=_params("arbitrary", "arbitrary"),
        name="gated_merge",
    )(*outs, *lses, u, rest, rest, wa, wc, ba, bc)


def _proj_residual_kernel(x_ref, a_ref, w_ref, o_ref):
    o_ref[...] = x_ref[...] + jnp.dot(a_ref[...], w_ref[...], preferred_element_type=F32)


def proj_residual(x, a, w, *, bm, bn):
    t, d = x.shape
    k = a.shape[1]
    bm, bn = min(bm, t), min(bn, d)
    return pl.pallas_call(
        _proj_residual_kernel,
        grid=(t // bm, d // bn),
        in_specs=[
            pl.BlockSpec((bm, bn), lambda i, j: (i, j)),
            pl.BlockSpec((bm, k), lambda i, j: (i, 0)),
            pl.BlockSpec((k, bn), lambda i, j: (0, j)),
        ],
        out_specs=pl.BlockSpec((bm, bn), lambda i, j: (i, j)),
        out_shape=jax.ShapeDtypeStruct((t, d), F32),
        compiler_params=_params("arbitrary", "arbitrary"),
        name="proj_residual",
    )(x, a, w)


def _swiglu_step(h, w1_ref, w3_ref, w2_ref):
    a = jnp.dot(h, w1_ref[...], preferred_element_type=F32)
    b = jnp.dot(h, w3_ref[...], preferred_element_type=F32)
    act = (a * jax.nn.sigmoid(a) * b).astype(BF16)
    return jnp.dot(act, w2_ref[...], preferred_element_type=F32)


def _ffn_kernel(x_ref, g_ref, w1_ref, w3_ref, w2_ref, o_ref, h_scr):
    @pl.when(pl.program_id(1) == 0)
    def _():
        _norm_rows_to(h_scr, x_ref, g_ref)
        o_ref[...] = x_ref[...]

    o_ref[...] += _swiglu_step(h_scr[...], w1_ref, w3_ref, w2_ref)


def dense_ffn(x, g, w1, w3, w2, *, bm, bf):
    t, d = x.shape
    f = w1.shape[1]
    bm, bf = min(bm, t), min(bf, f)
    return pl.pallas_call(
        _ffn_kernel,
        grid=(t // bm, f // bf),
        in_specs=[
            pl.BlockSpec((bm, d), lambda i, k: (i, 0)),
            pl.BlockSpec((1, d), lambda i, k: (0, 0)),
            pl.BlockSpec((d, bf), lambda i, k: (0, k)),
            pl.BlockSpec((d, bf), lambda i, k: (0, k)),
            pl.BlockSpec((bf, d), lambda i, k: (k, 0)),
        ],
        out_specs=pl.BlockSpec((bm, d), lambda i, k: (i, 0)),
        out_shape=jax.ShapeDtypeStruct((t, d), F32),
        scratch_shapes=[pltpu.VMEM((bm, d), BF16)],
        compiler_params=_params("arbitrary", "arbitrary"),
        name="dense_ffn",
    )(x, g, w1, w3, w2)


def _pack_bf16_pairs(v):
    half = v.shape[1] // 2
    bits = pltpu.bitcast(v, U32)
    bits = (bits + jnp.uint32(0x7FFF) + ((bits >> 16) & jnp.uint32(1))) & jnp.uint32(0xFFFF0000)
    return bits[:, half:] | (bits[:, :half] >> 16)


def _unpack_bf16_pairs(w):
    return pltpu.bitcast(w << 16, F32), pltpu.bitcast(w & jnp.uint32(0xFFFF0000), F32)


def _router_kernel(x_ref, g_ref, r_ref, tri_ref, hn_ref, dest_ref, gate_ref, cnt_ref, carry_scr, *, capacity):
    @pl.when(pl.program_id(0) == 0)
    def _():
        carry_scr[...] = jnp.zeros_like(carry_scr)

    h = _rms_rows(x_ref[...], g_ref[...])
    hn_ref[...] = _pack_bf16_pairs(h)
    logits = jnp.dot(h, r_ref[...], preferred_element_type=F32, precision=lax.Precision.HIGHEST)
    lane = lax.broadcasted_iota(I32, logits.shape, 1)
    logits = jnp.where(lane < N_EXPERTS, logits, NEG_INF)
    m1 = jnp.max(logits, axis=1, keepdims=True)
    i1 = jnp.min(jnp.where(logits == m1, lane, LANES), axis=1, keepdims=True)
    hot1 = lane == i1
    rest = jnp.where(hot1, NEG_INF, logits)
    m2 = jnp.max(rest, axis=1, keepdims=True)
    i2 = jnp.min(jnp.where(rest == m2, lane, LANES), axis=1, keepdims=True)
    hot2 = lane == i2
    e = jnp.exp(m2 - m1)
    w1 = 1.0 / (1.0 + e)
    w2 = e * w1
    sel = jnp.where(hot1, 1.0, 0.0) + jnp.where(hot2, 1.0, 0.0)
    rank = carry_scr[...] + jnp.dot(tri_ref[...], sel.astype(BF16), preferred_element_type=F32)
    r1 = jnp.sum(jnp.where(hot1, rank, 0.0), axis=1, keepdims=True).astype(I32)
    r2 = jnp.sum(jnp.where(hot2, rank, 0.0), axis=1, keepdims=True).astype(I32)
    carry_scr[...] = carry_scr[...] + jnp.sum(sel, axis=0, keepdims=True)
    dest_ref[...] = jnp.where(lane == 0, i1 * capacity + r1, jnp.where(lane == 1, i2 * capacity + r2, 0))
    gate_ref[...] = jnp.where(lane == 0, w1, jnp.where(lane == 1, w2, 0.0))
    cnt_ref[...] = jnp.broadcast_to(carry_scr[...], cnt_ref.shape)


def moe_router(x, g, router, *, bm):
    t, d = x.shape
    bm = min(bm, t)
    r_pad = jnp.zeros((d, LANES), F32).at[:, :N_EXPERTS].set(router)
    tri = (lax.broadcasted_iota(I32, (bm, bm), 0) > lax.broadcasted_iota(I32, (bm, bm), 1)).astype(BF16)
    return pl.pallas_call(
        functools.partial(_router_kernel, capacity=t),
        grid=(t // bm,),
        in_specs=[
            pl.BlockSpec((bm, d), lambda i: (i, 0)),
            pl.BlockSpec((1, d), lambda i: (0, 0)),
            pl.BlockSpec((d, LANES), lambda i: (0, 0)),
            pl.BlockSpec((bm, bm), lambda i: (0, 0)),
        ],
        out_specs=[
            pl.BlockSpec((bm, d // 2), lambda i: (i, 0)),
            pl.BlockSpec((bm, LANES), lambda i: (i, 0)),
            pl.BlockSpec((bm, LANES), lambda i: (i, 0)),
            pl.BlockSpec((8, LANES), lambda i: (0, 0)),
        ],
        out_shape=[
            jax.ShapeDtypeStruct((t, d // 2), U32),
            jax.ShapeDtypeStruct((t, LANES), I32),
            jax.ShapeDtypeStruct((t, LANES), F32),
            jax.ShapeDtypeStruct((8, LANES), F32),
        ],
        scratch_shapes=[pltpu.VMEM((1, LANES), F32)],
        compiler_params=_params("arbitrary"),
        name="moe_router",
    )(x, g, r_pad, tri)


def _row_copy(src_ref, src_row, dst_ref, dst_row, sem):
    return pltpu.make_async_copy(src_ref.at[pl.ds(src_row, 1)], dst_ref.at[pl.ds(dst_row, 1)], sem)


def _dispatch_kernel(cnt_ref, off_ref, d0_ref, d1_ref, hn_ref, xs_ref, zero_scr, sem, pad_sem, *, row_tile):
    bm = hn_ref.shape[0]
    n_rows = xs_ref.shape[0]

    def pad_copies(do):
        for e in range(N_EXPERTS):
            cnt = cnt_ref[e]
            pad = (-cnt) & (row_tile - 1)
            head = pad & (SUBLANES - 1)
            off = off_ref[e] + cnt
            for b in range(SUBLANES - 1):
                @pl.when(b < head)
                def _(off=off, b=b):
                    do(pltpu.make_async_copy(zero_scr.at[pl.ds(0, 1)], xs_ref.at[pl.ds(off + b, 1)], pad_sem))
            off = off + head
            size = SUBLANES
            while size < row_tile:
                @pl.when((pad & size) != 0)
                def _(off=off, size=size):
                    dst = xs_ref.at[pl.ds(pl.multiple_of(off, SUBLANES), size)]
                    do(pltpu.make_async_copy(zero_scr.at[pl.ds(0, size)], dst, pad_sem))
                off = off + (pad & size)
                size *= 2
        used = off_ref[N_EXPERTS]
        for tile in range(n_rows // row_tile - N_EXPERTS, n_rows // row_tile):
            for piece in range(2):
                @pl.when(tile * row_tile >= used)
                def _(tile=tile, piece=piece):
                    dst = xs_ref.at[pl.ds(tile * row_tile + piece * (row_tile // 2), row_tile // 2)]
                    do(pltpu.make_async_copy(zero_scr, dst, pad_sem))

    @pl.when(pl.program_id(0) == 0)
    def _():
        zero_scr[...] = jnp.zeros_like(zero_scr)
        pad_copies(lambda cp: cp.start())
        pad_copies(lambda cp: cp.wait())

    def issue(t, carry):
        _row_copy(hn_ref, t, xs_ref, d0_ref[0, 0, t], sem).start()
        _row_copy(hn_ref, t, xs_ref, d1_ref[0, 0, t], sem).start()
        return carry

    def drain(t, carry):
        _row_copy(hn_ref, 0, xs_ref, 0, sem).wait()
        _row_copy(hn_ref, 0, xs_ref, 0, sem).wait()
        return carry

    lax.fori_loop(0, bm, issue, 0)
    lax.fori_loop(0, bm, drain, 0)


def moe_dispatch(counts, offsets, dest0, dest1, hn, *, bm, row_tile, n_rows):
    t, d = hn.shape
    bm = min(bm, t)
    nt = t // bm
    idx_spec = pl.BlockSpec((1, 1, bm), lambda i, cnt, off: (i, 0, 0), memory_space=pltpu.SMEM)
    return pl.pallas_call(
        functools.partial(_dispatch_kernel, row_tile=row_tile),
        grid_spec=pltpu.PrefetchScalarGridSpec(
            num_scalar_prefetch=2,
            grid=(nt,),
            in_specs=[idx_spec, idx_spec, pl.BlockSpec((bm, d), lambda i, cnt, off: (i, 0))],
            out_specs=pl.BlockSpec(memory_space=pl.ANY),
            scratch_shapes=[pltpu.VMEM((row_tile // 2, d), hn.dtype),
                            pltpu.SemaphoreType.DMA(()), pltpu.SemaphoreType.DMA(())],
        ),
        out_shape=jax.ShapeDtypeStruct((n_rows, d), hn.dtype),
        compiler_params=_params("arbitrary"),
        name="moe_dispatch",
    )(counts, offsets, dest0.reshape(nt, 1, bm), dest1.reshape(nt, 1, bm), hn)


def _moe_ffn_kernel(te_ref, tv_ref, x_ref, w1_ref, w3_ref, w2_ref, o_ref, h_scr, acc_scr):
    k = pl.program_id(1)
    valid = tv_ref[pl.program_id(0)] != 0
    half = h_scr.shape[1] // 2

    @pl.when(valid & (k == 0))
    def _():
        lo, hi = _unpack_bf16_pairs(x_ref[...])
        h_scr[:, :half] = lo.astype(h_scr.dtype)
        h_scr[:, half:] = hi.astype(h_scr.dtype)
        acc_scr[...] = jnp.zeros_like(acc_scr)

    @pl.when(valid)
    def _():
        acc_scr[...] += _swiglu_step(h_scr[...], w1_ref, w3_ref, w2_ref)

    @pl.when(valid & (k == pl.num_programs(1) - 1))
    def _():
        o_ref[...] = _pack_bf16_pairs(acc_scr[...])

    @pl.when(jnp.logical_not(valid) & (k == 0))
    def _():
        o_ref[...] = jnp.zeros_like(o_ref)


def moe_grouped_ffn(tile_expert, tile_valid, xs, w1, w3, w2, *, row_tile, bf):
    rows, half = xs.shape
    d = 2 * half
    f = w1.shape[2]
    bf = min(bf, f)
    nf = f // bf
    n_steps = tile_expert.shape[0]

    def fk(s, k, tv):
        return jnp.where(tv[s] != 0, k, nf - 1)

    return pl.pallas_call(
        _moe_ffn_kernel,
        grid_spec=pltpu.PrefetchScalarGridSpec(
            num_scalar_prefetch=2,
            grid=(n_steps, nf),
            in_specs=[
                pl.BlockSpec((row_tile, half), lambda s, k, te, tv: (s, 0)),
                pl.BlockSpec((None, d, bf), lambda s, k, te, tv: (te[s], 0, fk(s, k, tv))),
                pl.BlockSpec((None, d, bf), lambda s, k, te, tv: (te[s], 0, fk(s, k, tv))),
                pl.BlockSpec((None, bf, d), lambda s, k, te, tv: (te[s], fk(s, k, tv), 0)),
            ],
            out_specs=pl.BlockSpec((row_tile, half), lambda s, k, te, tv: (s, 0)),
            scratch_shapes=[pltpu.VMEM((row_tile, d), BF16), pltpu.VMEM((row_tile, d), F32)],
        ),
        out_shape=jax.ShapeDtypeStruct((rows, half), U32),
        compiler_params=_params("arbitrary", "arbitrary"),
        name="moe_grouped_ffn",
    )(tile_expert, tile_valid, xs, w1, w3, w2)


def _combine_kernel(d0_ref, d1_ref, x_ref, gate_ref, ys_ref, o_ref, buf, sem):
    bm = x_ref.shape[0]

    def issue(t, carry):
        _row_copy(ys_ref, d0_ref[0, 0, t], buf.at[0], t, sem).start()
        _row_copy(ys_ref, d1_ref[0, 0, t], buf.at[1], t, sem).start()
        return carry

    def drain(t, carry):
        _row_copy(ys_ref, 0, buf.at[0], 0, sem).wait()
        _row_copy(ys_ref, 0, buf.at[1], 0, sem).wait()
        return carry

    lax.fori_loop(0, bm, issue, 0)
    lax.fori_loop(0, bm, drain, 0)
    gates = gate_ref[...]
    half = buf.shape[2]
    lo0, hi0 = _unpack_bf16_pairs(buf[0])
    lo1, hi1 = _unpack_bf16_pairs(buf[1])
    o_ref[:, :half] = x_ref[:, :half] + gates[:, 0:1] * lo0 + gates[:, 1:2] * lo1
    o_ref[:, half:] = x_ref[:, half:] + gates[:, 0:1] * hi0 + gates[:, 1:2] * hi1


def moe_combine(dest0, dest1, x, gates, ys, *, bm):
    t, d = x.shape
    bm = min(bm, t)
    nt = t // bm
    idx_spec = pl.BlockSpec((1, 1, bm), lambda i: (i, 0, 0), memory_space=pltpu.SMEM)
    return pl.pallas_call(
        _combine_kernel,
        grid=(nt,),
        in_specs=[idx_spec, idx_spec,
                  pl.BlockSpec((bm, d), lambda i: (i, 0)),
                  pl.BlockSpec((bm, LANES), lambda i: (i, 0)),
                  pl.BlockSpec(memory_space=pl.ANY)],
        out_specs=pl.BlockSpec((bm, d), lambda i: (i, 0)),
        out_shape=jax.ShapeDtypeStruct((t, d), F32),
        scratch_shapes=[pltpu.VMEM((2, bm, d // 2), ys.dtype), pltpu.SemaphoreType.DMA(())],
        compiler_params=_params("arbitrary"),
        name="moe_combine",
    )(dest0.reshape(nt, 1, bm), dest1.reshape(nt, 1, bm), x, gates, ys)


def _tile_tables(counts, *, row_tile, n_steps):
    tiles = (counts + row_tile - 1) // row_tile
    ends = jnp.cumsum(tiles)
    offsets = jnp.concatenate([jnp.zeros((1,), I32), ends * row_tile]).astype(I32)
    step = jnp.arange(n_steps, dtype=I32)
    valid = (step < ends[-1]).astype(I32)
    expert = jnp.sum((jnp.minimum(step, ends[-1] - 1)[:, None] >= ends[None, :]).astype(I32), axis=1)
    return offsets, expert.astype(I32), valid


def moe_ffn(x, g, router, w1, w3, w2, *, bm, row_tile, bf):
    t, d = x.shape
    row_tile = min(row_tile, t)
    hn, dest, gates, cnt = moe_router(x, g, router, bm=bm)
    counts = cnt[0, :N_EXPERTS].astype(I32)
    n_steps = (2 * t) // row_tile + N_EXPERTS
    offsets, tile_expert, tile_valid = _tile_tables(counts, row_tile=row_tile, n_steps=n_steps)
    dest0, dest1 = (offsets[dest[:, c] // t] + dest[:, c] % t for c in range(2))
    xs = moe_dispatch(counts, offsets, dest0, dest1, hn, bm=bm, row_tile=row_tile, n_rows=n_steps * row_tile)
    ys = moe_grouped_ffn(tile_expert, tile_valid, xs, w1, w3, w2, row_tile=row_tile, bf=bf)
    return moe_combine(dest0, dest1, x, gates, ys, bm=bm)


def kernel(x, norm_mix, w_in, q_norm, k_norm, w_attn_o, conv_w, conv_b, conv_ln_g, conv_ln_b, w_conv_o, gate_b,
           w_out, norm_ffn, ffn_w1, ffn_w3, ffn_w2, router, moe_w1, moe_w3, moe_w2):
    batch, seq, d = x.shape
    depth = w_in.shape[0]
    ch = conv_w.shape[-1]
    t = batch * seq
    dilations = tuple(dil for _, dil in ATTN_GROUPS)
    strided = tuple(dil for dil in dilations if dil > 1)
    tabs = [rope_table(seq, dil) for dil in dilations]
    row = lambda v: v.reshape(1, -1).astype(F32)
    xt = x.reshape(t, d)
    for l in range(depth):
        w_in_l = w_in[l]
        hs = norm_permute(xt, row(norm_mix[l]), batch=batch, seq=seq, dilations=strided, bm=512)
        h_by_dil = dict(zip((1,) + strided, hs))
        outs, lses = [], []
        for grp, dil in enumerate(dilations):
            cols = [slice(part * ATTN_WIDTH + grp * GROUP_WIDTH, part * ATTN_WIDTH + (grp + 1) * GROUP_WIDTH)
                    for part in range(3)]
            w_g = jnp.concatenate([w_in_l[:, c] for c in cols], axis=1).astype(BF16)
            qkv_g = qkv_proj(h_by_dil[dil], w_g, tabs[grp], row(q_norm[l]), row(k_norm[l]), seq=seq, bm=1024)
            o_g, lse_g = dilated_attention(qkv_g, group=grp, batch=batch, seq=seq, lq=2048 // max(dil, 4))
            outs.append(o_g)
            lses.append(lse_g)
        rest = matmul(h_by_dil[1], w_in_l[:, 3 * ATTN_WIDTH:].astype(BF16), bm=1024, bn=2048)
        u = conv_branch(rest, conv_w[l].reshape(CONV_WIDTH, ch), row(conv_b[l]), row(conv_ln_g[l]),
                        row(conv_ln_b[l]), batch=batch, seq=seq, ch=ch, bs=256)
        merged = gated_merge(outs, lses, u, rest, w_attn_o[l].astype(BF16), w_conv_o[l].astype(BF16),
                             row(gate_b[l, :d]), row(gate_b[l, d:]), d=d, bm=1024, bn=512)
        xt = proj_residual(xt, merged, w_out[l].astype(BF16), bm=512, bn=2048)
        g_ffn = row(norm_ffn[l])
        i = l // 2
        if l % 2 == 0:
            xt = dense_ffn(xt, g_ffn, ffn_w1[i].astype(BF16), ffn_w3[i].astype(BF16), ffn_w2[i].astype(BF16),
                           bm=512, bf=1024)
        else:
            xt = moe_ffn(xt, g_ffn, router[i], moe_w1[i].astype(BF16), moe_w3[i].astype(BF16),
                         moe_w2[i].astype(BF16), bm=512, row_tile=512, bf=1024)
    return xt.reshape(batch, seq, d)
```
